```python
import jax, jax.numpy as jnp
from jax import lax
import numpy as np

D_MODEL = 1024
BATCH = 16
SEQ = 2048
DEPTH = 4

RET_HEADS = 4
RET_QK_DIM = 128
RET_V_DIM = 256
RET_CHUNK = 128
SB_HEADS = 8
SB_HEAD_DIM = 128
SB_BLOCK = 128
D_FF = 4 * D_MODEL

ROPE_BASE = 10000.0
NORM_EPS = 1e-6
GN_EPS = 1e-5

RET_QK = RET_HEADS * RET_QK_DIM
RET_V = RET_HEADS * RET_V_DIM
SB_W = SB_HEADS * SB_HEAD_DIM
IN_SIZES = (RET_QK, RET_QK, RET_V, RET_V, SB_W, SB_W, SB_W, D_MODEL, D_MODEL)
D_IN = RET_QK * 2 + RET_V * 2 + SB_W * 3 + D_MODEL * 2

kernel_name = 'hybrid_retention_stickbreaking_gated'


def rmsnorm(x, gain):
    xf = x.astype(jnp.float32)
    y = xf * lax.rsqrt(jnp.mean(xf * xf, axis=-1, keepdims=True) + NORM_EPS)
    return (y * gain.astype(jnp.float32)).astype(x.dtype)


def rotary(t, pos):
    half = t.shape[-1] // 2
    inv_freq = ROPE_BASE ** (-jnp.arange(half, dtype=jnp.float32) / half)
    ang = pos.astype(jnp.float32)[:, None] * inv_freq[None, :]
    cos = jnp.cos(ang)[None, :, None, :]
    sin = jnp.sin(ang)[None, :, None, :]
    t1, t2 = t[..., :half], t[..., half:]
    return jnp.concatenate([t1 * cos - t2 * sin, t1 * sin + t2 * cos], axis=-1)


def retention(q, k, v, log_gamma):
    B, S, H, _ = q.shape
    C = RET_CHUNK
    N = S // C
    idx = jnp.arange(C, dtype=jnp.float32)
    diff = idx[:, None] - idx[None, :]
    decay = jnp.where(diff[None] >= 0,
                      jnp.exp(log_gamma[:, None, None] * jnp.maximum(diff, 0.0)[None]), 0.0)
    q_decay = jnp.exp(log_gamma[None, :] * (idx[:, None] + 1.0))
    k_decay = jnp.exp(log_gamma[None, :] * (C - 1.0 - idx[:, None]))
    chunk_decay = jnp.exp(log_gamma * C)

    def to_chunks(t):
        return t.reshape(B, N, C, H, t.shape[-1]).transpose(1, 0, 2, 3, 4)

    def step(state, qkv):
        qc, kc, vc = qkv
        scores = jnp.einsum('bchd,bshd->bhcs', qc, kc) * decay[None]
        inner = jnp.einsum('bhcs,bshe->bche', scores, vc)
        cross = jnp.einsum('bchd,bhde->bche', qc, state) * q_decay[None, :, :, None]
        state = state * chunk_decay[None, :, None, None] + jnp.einsum(
            'bshd,bshe->bhde', kc * k_decay[None, :, :, None], vc)
        return state, inner + cross

    init = jnp.zeros((B, H, q.shape[-1], v.shape[-1]), jnp.float32)
    _, out = lax.scan(step, init, (to_chunks(q), to_chunks(k), to_chunks(v)))
    return out.transpose(1, 0, 2, 3, 4).reshape(B, S, H, v.shape[-1])


def head_group_norm(y):
    mu = jnp.mean(y, axis=-1, keepdims=True)
    var = jnp.mean(jnp.square(y - mu), axis=-1, keepdims=True)
    return (y - mu) * lax.rsqrt(var + GN_EPS)


def stick_breaking(q, k, v):
    S = q.shape[1]
    scale = q.shape[-1] ** -0.5
    outs = []
    for start in range(0, S, SB_BLOCK):
        end = start + SB_BLOCK
        qb = q[:, start:end]
        kb = k[:, :end]
        vb = v[:, :end]
        z = jnp.einsum('bthd,bshd->bhts', qb, kb) * scale
        t_idx = start + jnp.arange(SB_BLOCK)
        s_idx = jnp.arange(end)
        causal = (s_idx[None, :] < t_idx[:, None])[None, None]
        log_beta = jax.nn.log_sigmoid(z)
        log_1m_beta = jnp.where(causal, log_beta - z, 0.0)
        after = lax.cumsum(log_1m_beta, axis=3, reverse=True) - log_1m_beta
        weights = jnp.where(causal, jnp.exp(log_beta + after), 0.0)
        outs.append(jnp.einsum('bhts,bshe->bthe', weights, vb))
    return jnp.concatenate(outs, axis=1)


def setup_inputs(seed: int = 0) -> dict:
    key = jax.random.key(seed)
    ks = jax.random.split(key, 10)
    f32 = jnp.float32
    x = jax.random.normal(ks[0], (BATCH, SEQ, D_MODEL), f32)
    w_in = jax.random.normal(ks[1], (DEPTH, D_MODEL, D_IN), f32) * D_MODEL ** -0.5
    p_ret = jax.random.normal(ks[2], (DEPTH, RET_V, D_MODEL), f32) * RET_V ** -0.5
    p_sb = jax.random.normal(ks[3], (DEPTH, SB_W, D_MODEL), f32) * SB_W ** -0.5
    w_out = jax.random.normal(ks[4], (DEPTH, D_MODEL, D_MODEL), f32) * D_MODEL ** -0.5
    w_up = jax.random.normal(ks[5], (DEPTH, D_MODEL, D_FF), f32) * D_MODEL ** -0.5
    w_down = jax.random.normal(ks[6], (DEPTH, D_FF, D_MODEL), f32) * D_FF ** -0.5
    g_mix = 1.0 + 0.02 * jax.random.normal(ks[7], (DEPTH, D_MODEL), f32)
    g_mlp = 1.0 + 0.02 * jax.random.normal(ks[8], (DEPTH, D_MODEL), f32)
    g_final = 1.0 + 0.02 * jax.random.normal(ks[9], (D_MODEL,), f32)
    return {'x': x, 'w_in': w_in, 'p_ret': p_ret, 'p_sb': p_sb, 'w_out': w_out,
            'w_up': w_up, 'w_down': w_down, 'g_mix': g_mix, 'g_mlp': g_mlp,
            'g_final': g_final}


def reference(x, w_in, p_ret, p_sb, w_out, w_up, w_down, g_mix, g_mlp, g_final):
    B, S, _ = x.shape
    dt = x.dtype
    pos = jnp.arange(S, dtype=jnp.int32)
    log_gamma = jnp.log1p(-(2.0 ** (-5.0 - jnp.arange(RET_HEADS, dtype=jnp.float32))))
    splits = [int(v) for v in np.cumsum(IN_SIZES)[:-1]]
    for l in range(DEPTH):
        h = rmsnorm(x, g_mix[l])
        proj = jnp.matmul(h, w_in[l]).astype(jnp.float32)
        q_r, k_r, v_r, g_r, q_s, k_s, v_s, gate_a, gate_b = jnp.split(proj, splits, axis=-1)
        q_r = rotary(q_r.reshape(B, S, RET_HEADS, RET_QK_DIM), pos)
        k_r = rotary(k_r.reshape(B, S, RET_HEADS, RET_QK_DIM), pos) * (RET_QK_DIM ** -0.5)
        ret = retention(q_r, k_r, v_r.reshape(B, S, RET_HEADS, RET_V_DIM), log_gamma)
        ret = head_group_norm(ret).reshape(B, S, RET_V) * jax.nn.silu(g_r)
        sb = stick_breaking(q_s.reshape(B, S, SB_HEADS, SB_HEAD_DIM),
                            k_s.reshape(B, S, SB_HEADS, SB_HEAD_DIM),
                            v_s.reshape(B, S, SB_HEADS, SB_HEAD_DIM)).reshape(B, S, SB_W)
        branch_a = jnp.matmul(ret.astype(dt), p_ret[l]).astype(jnp.float32)
        branch_b = jnp.matmul(sb.astype(dt), p_sb[l]).astype(jnp.float32)
        merged = jax.nn.sigmoid(gate_a) * branch_a + jax.nn.sigmoid(gate_b) * branch_b
        x = x + jnp.matmul(merged.astype(dt), w_out[l]).astype(dt)
        h = rmsnorm(x, g_mlp[l])
        up = jnp.square(jax.nn.relu(jnp.matmul(h, w_up[l])))
        x = x + jnp.matmul(up, w_down[l]).astype(dt)
    return rmsnorm(x, g_final)
```

```python
import functools

import numpy as np
import jax
import jax.numpy as jnp
from jax import lax
from jax.experimental import pallas as pl
from jax.experimental.pallas import tpu as pltpu

D_MODEL = 1024
RET_HEADS = 4
RET_QK_DIM = 128
RET_V_DIM = 256
RET_CHUNK = 128
SB_HEADS = 8
SB_HEAD_DIM = 128
D_FF = 4 * D_MODEL
ROPE_BASE = 10000.0
NORM_EPS = 1e-6
GN_EPS = 1e-5

RET_QK = RET_HEADS * RET_QK_DIM
RET_V = RET_HEADS * RET_V_DIM
SB_W = SB_HEADS * SB_HEAD_DIM
PA_W = 2 * RET_QK + RET_V + 3 * SB_W
PB_W = RET_V + 2 * D_MODEL

VMEM_LIMIT_BYTES = 56 * 1024 * 1024

BF16 = jnp.bfloat16
F32 = jnp.float32


def _params(semantics):
    return pltpu.CompilerParams(dimension_semantics=semantics,
                                vmem_limit_bytes=VMEM_LIMIT_BYTES)


def _dot(a, b):
    return jnp.dot(a, b, preferred_element_type=F32)


def _dot_nt(a, b):
    return lax.dot_general(a, b, (((1,), (1,)), ((), ())), preferred_element_type=F32)


def _dot_tn(a, b):
    return lax.dot_general(a, b, (((0,), (0,)), ((), ())), preferred_element_type=F32)


def _rms(xf, gain):
    ms = jnp.mean(xf * xf, axis=-1, keepdims=True)
    return xf * lax.rsqrt(ms + NORM_EPS) * gain


def _norm_proj_kernel(x_ref, g_ref, w_ref, cos_ref, sin_ref, o_ref, h_ref, *, rotary):
    j = pl.program_id(1)

    @pl.when(j == 0)
    def _():
        h_ref[...] = _rms(x_ref[...], g_ref[...]).astype(BF16)

    acc = _dot(h_ref[...], w_ref[...])

    if rotary:
        @pl.when(j == 0)
        def _():
            cos = cos_ref[...]
            sin = sin_ref[...]
            n_groups = o_ref.shape[1] // RET_QK_DIM
            for g in range(n_groups):
                sl = slice(g * RET_QK_DIM, (g + 1) * RET_QK_DIM)
                t = acc[:, sl]
                r = t * cos + pltpu.roll(t, RET_QK_DIM // 2, axis=1) * sin
                if g >= RET_HEADS:
                    r = r * (RET_QK_DIM ** -0.5)
                o_ref[:, sl] = r.astype(o_ref.dtype)

        @pl.when(j != 0)
        def _():
            o_ref[...] = acc.astype(o_ref.dtype)
    else:
        o_ref[...] = acc.astype(o_ref.dtype)


def _norm_proj(x2, gain, w, cos, sin, *, out_dtype, rotary, tm=1024, tn=1024):
    m, d = x2.shape
    n = w.shape[1]
    seq = cos.shape[0]
    pos_blocks = seq // tm
    return pl.pallas_call(
        functools.partial(_norm_proj_kernel, rotary=rotary),
        out_shape=jax.ShapeDtypeStruct((m, n), out_dtype),
        grid=(m // tm, n // tn),
        in_specs=[
            pl.BlockSpec((tm, d), lambda i, j: (i, 0)),
            pl.BlockSpec((1, d), lambda i, j: (0, 0)),
            pl.BlockSpec((d, tn), lambda i, j: (0, j)),
            pl.BlockSpec((tm, RET_QK_DIM), lambda i, j: (i % pos_blocks, 0)),
            pl.BlockSpec((tm, RET_QK_DIM), lambda i, j: (i % pos_blocks, 0)),
        ],
        out_specs=pl.BlockSpec((tm, tn), lambda i, j: (i, j)),
        scratch_shapes=[pltpu.VMEM((tm, d), BF16)],
        compiler_params=_params(("parallel", "arbitrary")),
        name="norm_proj_rot" if rotary else "norm_proj",
    )(x2, gain, w, cos, sin)


def _retention_kernel(q_ref, k_ref, v_ref, g_ref, decay_ref, qdec_ref, kdec_ref, o_ref,
                      state_ref, *, chunk_decay):
    n = pl.program_id(1)

    @pl.when(n == 0)
    def _():
        state_ref[...] = jnp.zeros_like(state_ref)

    for h in range(RET_HEADS):
        qk = slice(h * RET_QK_DIM, (h + 1) * RET_QK_DIM)
        vv = slice(h * RET_V_DIM, (h + 1) * RET_V_DIM)
        q = q_ref[:, qk]
        k = k_ref[:, qk]
        v = v_ref[:, vv]
        state = state_ref[h]
        scores = _dot_nt(q, k) * decay_ref[h]
        inner = _dot(scores.astype(BF16), v)
        cross = _dot(q, state.astype(BF16)) * qdec_ref[:, vv]
        kd = (k.astype(F32) * kdec_ref[:, qk]).astype(BF16)
        state_ref[h] = state * chunk_decay[h] + _dot_tn(kd, v)
        y = inner + cross
        mu = jnp.mean(y, axis=-1, keepdims=True)
        yc = y - mu
        var = jnp.mean(yc * yc, axis=-1, keepdims=True)
        yn = yc * lax.rsqrt(var + GN_EPS)
        gate = g_ref[:, vv]
        o_ref[:, vv] = (yn * (gate * jax.nn.sigmoid(gate))).astype(o_ref.dtype)


def _retention(pa, pb, decay, qdec, kdec, chunk_decay, *, batch, seq):
    c = RET_CHUNK
    n_chunks = seq // c
    m = pa.shape[0]
    row = lambda b, n: b * n_chunks + n
    return pl.pallas_call(
        functools.partial(_retention_kernel, chunk_decay=chunk_decay),
        out_shape=jax.ShapeDtypeStruct((m, RET_V), BF16),
        grid=(batch, n_chunks),
        in_specs=[
            pl.BlockSpec((c, RET_QK), lambda b, n: (row(b, n), 0)),
            pl.BlockSpec((c, RET_QK), lambda b, n: (row(b, n), 1)),
            pl.BlockSpec((c, RET_V), lambda b, n: (row(b, n), 1)),
            pl.BlockSpec((c, RET_V), lambda b, n: (row(b, n), 0)),
            pl.BlockSpec((RET_HEADS, c, c), lambda b, n: (0, 0, 0)),
            pl.BlockSpec((c, RET_V), lambda b, n: (0, 0)),
            pl.BlockSpec((c, RET_QK), lambda b, n: (0, 0)),
        ],
        out_specs=pl.BlockSpec((c, RET_V), lambda b, n: (row(b, n), 0)),
        scratch_shapes=[pltpu.VMEM((RET_HEADS, RET_QK_DIM, RET_V_DIM), F32)],
        compiler_params=_params(("parallel", "arbitrary")),
        name="retention",
    )(pa, pa, pa, pb, decay, qdec, kdec)


def _sb_block(q, kb, vb, tri, carry, acc, mask):
    z = _dot_nt(q, kb) * (SB_HEAD_DIM ** -0.5)
    lse = jnp.log(1.0 + jnp.exp(-jnp.abs(z)))
    log_beta = jnp.minimum(z, 0.0) - lse
    log_1m = log_beta - z
    if mask is not None:
        log_1m = jnp.where(mask, log_1m, 0.0)
    hi = log_1m.astype(BF16)
    lo = (log_1m - hi.astype(F32)).astype(BF16)
    after = _dot(hi, tri) + _dot(lo, tri)
    w = jnp.exp(log_beta + after + carry)
    if mask is not None:
        w = jnp.where(mask, w, 0.0)
    acc = acc + _dot(w.astype(BF16), vb)
    carry = carry + jnp.sum(log_1m, axis=-1, keepdims=True)
    return carry, acc


def _sb_kernel(q_ref, k_ref, v_ref, tri_ref, o_ref, *, tb):
    ti = pl.program_id(2)
    q = q_ref[...]
    tri = tri_ref[...]
    rows = lax.broadcasted_iota(jnp.int32, (tb, tb), 0)
    cols = lax.broadcasted_iota(jnp.int32, (tb, tb), 1)
    start = pl.multiple_of(ti * tb, tb)
    carry = jnp.zeros((tb, 1), F32)
    acc = jnp.zeros((tb, SB_HEAD_DIM), F32)
    carry, acc = _sb_block(q, k_ref[pl.ds(start, tb), :], v_ref[pl.ds(start, tb), :], tri,
                           carry, acc, cols < rows)

    def body(i, c):
        off = pl.multiple_of((ti - 1 - i) * tb, tb)
        return _sb_block(q, k_ref[pl.ds(off, tb), :], v_ref[pl.ds(off, tb), :], tri,
                         c[0], c[1], None)

    carry, acc = lax.fori_loop(0, ti, body, (carry, acc))
    o_ref[...] = acc.astype(o_ref.dtype)


def _stick_breaking(pa, tri, *, batch, seq, tb=256):
    m = pa.shape[0]
    t_blocks = seq // tb
    q_col = (2 * RET_QK + RET_V) // SB_HEAD_DIM
    k_col = q_col + SB_HEADS
    v_col = k_col + SB_HEADS
    return pl.pallas_call(
        functools.partial(_sb_kernel, tb=tb),
        out_shape=jax.ShapeDtypeStruct((m, SB_W), BF16),
        grid=(batch, SB_HEADS, t_blocks),
        in_specs=[
            pl.BlockSpec((tb, SB_HEAD_DIM), lambda b, h, t: (b * t_blocks + t, q_col + h)),
            pl.BlockSpec((seq, SB_HEAD_DIM), lambda b, h, t: (b, k_col + h)),
            pl.BlockSpec((seq, SB_HEAD_DIM), lambda b, h, t: (b, v_col + h)),
            pl.BlockSpec((tb, tb), lambda b, h, t: (0, 0)),
        ],
        out_specs=pl.BlockSpec((tb, SB_HEAD_DIM), lambda b, h, t: (b * t_blocks + t, h)),
        compiler_params=_params(("parallel", "parallel", "arbitrary")),
        name="stick_breaking",
    )(pa, pa, pa, tri)


def _merge_kernel(ret_ref, sb_ref, ga_ref, gb_ref, x_ref, pr_ref, ps_ref, wo_ref, o_ref):
    branch_a = _dot(ret_ref[...], pr_ref[...])
    branch_b = _dot(sb_ref[...], ps_ref[...])
    merged = jax.nn.sigmoid(ga_ref[...]) * branch_a + jax.nn.sigmoid(gb_ref[...]) * branch_b
    o_ref[...] = x_ref[...] + _dot(merged.astype(BF16), wo_ref[...])


def _merge(ret, sb, pb, x2, p_ret, p_sb, w_out, *, tm=512):
    m, d = x2.shape
    row = lambda i: (i, 0)
    const = lambda i: (0, 0)
    return pl.pallas_call(
        _merge_kernel,
        out_shape=jax.ShapeDtypeStruct((m, d), F32),
        grid=(m // tm,),
        in_specs=[
            pl.BlockSpec((tm, RET_V), row),
            pl.BlockSpec((tm, SB_W), row),
            pl.BlockSpec((tm, d), lambda i: (i, 1)),
            pl.BlockSpec((tm, d), lambda i: (i, 2)),
            pl.BlockSpec((tm, d), row),
            pl.BlockSpec((RET_V, d), const),
            pl.BlockSpec((SB_W, d), const),
            pl.BlockSpec((d, d), const),
        ],
        out_specs=pl.BlockSpec((tm, d), row),
        compiler_params=_params(("parallel",)),
        name="gate_merge",
    )(ret, sb, pb, pb, x2, p_ret, p_sb, w_out)


def _mlp_kernel(x_ref, g_ref, wu_ref, wd_ref, o_ref, h_ref):
    j = pl.program_id(1)

    @pl.when(j == 0)
    def _():
        xf = x_ref[...]
        h_ref[...] = _rms(xf, g_ref[...]).astype(BF16)
        o_ref[...] = xf

    up = jnp.maximum(_dot(h_ref[...], wu_ref[...]), 0.0)
    o_ref[...] += _dot((up * up).astype(BF16), wd_ref[...])


def _mlp(x2, gain, w_up, w_down, *, tm=1024, tf=512):
    m, d = x2.shape
    ff = w_up.shape[1]
    return pl.pallas_call(
        _mlp_kernel,
        out_shape=jax.ShapeDtypeStruct((m, d), F32),
        grid=(m // tm, ff // tf),
        in_specs=[
            pl.BlockSpec((tm, d), lambda i, j: (i, 0)),
            pl.BlockSpec((1, d), lambda i, j: (0, 0)),
            pl.BlockSpec((d, tf), lambda i, j: (0, j)),
            pl.BlockSpec((tf, d), lambda i, j: (j, 0)),
        ],
        out_specs=pl.BlockSpec((tm, d), lambda i, j: (i, 0)),
        scratch_shapes=[pltpu.VMEM((tm, d), BF16)],
        compiler_params=_params(("parallel", "arbitrary")),
        name="relu2_mlp",
    )(x2, gain, w_up, w_down)


def _final_norm_kernel(x_ref, g_ref, o_ref):
    o_ref[...] = _rms(x_ref[...], g_ref[...])


def _final_norm(x2, gain, *, tm=1024):
    m, d = x2.shape
    return pl.pallas_call(
        _final_norm_kernel,
        out_shape=jax.ShapeDtypeStruct((m, d), F32),
        grid=(m // tm,),
        in_specs=[pl.BlockSpec((tm, d), lambda i: (i, 0)),
                  pl.BlockSpec((1, d), lambda i: (0, 0))],
        out_specs=pl.BlockSpec((tm, d), lambda i: (i, 0)),
        compiler_params=_params(("parallel",)),
        name="final_norm",
    )(x2, gain)


def _tables(seq, tb):
    half = RET_QK_DIM // 2
    inv_freq = ROPE_BASE ** (-jnp.arange(half, dtype=F32) / half)
    ang = jnp.arange(seq, dtype=F32)[:, None] * inv_freq[None, :]
    cos = jnp.cos(ang)
    sin = jnp.sin(ang)
    cos_t = jnp.concatenate([cos, cos], axis=-1)
    sin_t = jnp.concatenate([-sin, sin], axis=-1)

    log_gamma = jnp.log1p(-(2.0 ** (-5.0 - jnp.arange(RET_HEADS, dtype=F32))))
    c = RET_CHUNK
    idx = jnp.arange(c, dtype=F32)
    diff = idx[:, None] - idx[None, :]
    decay = jnp.where(diff[None] >= 0,
                      jnp.exp(log_gamma[:, None, None] * jnp.maximum(diff, 0.0)[None]), 0.0)
    q_decay = jnp.exp(log_gamma[None, :] * (idx[:, None] + 1.0))
    k_decay = jnp.exp(log_gamma[None, :] * (c - 1.0 - idx[:, None]))
    qdec = jnp.repeat(q_decay, RET_V_DIM, axis=1)
    kdec = jnp.repeat(k_decay, RET_QK_DIM, axis=1)
    gamma = 1.0 - 2.0 ** (-5.0 - np.arange(RET_HEADS, dtype=np.float64))
    chunk_decay = tuple(float(g) for g in gamma ** c)

    s_idx = jnp.arange(tb)
    tri = (s_idx[:, None] > s_idx[None, :]).astype(BF16)
    return cos_t, sin_t, decay, qdec, kdec, chunk_decay, tri


def kernel(x, w_in, p_ret, p_sb, w_out, w_up, w_down, g_mix, g_mlp, g_final):
    batch, seq, d = x.shape
    depth = w_in.shape[0]
    sb_tb = 256
    cos_t, sin_t, decay, qdec, kdec, chunk_decay, tri = _tables(seq, sb_tb)

    o = np.cumsum((0, RET_QK, RET_QK, RET_V, RET_V, SB_W, SB_W, SB_W, D_MODEL, D_MODEL))
    w_a = jnp.concatenate([w_in[:, :, o[0]:o[3]], w_in[:, :, o[4]:o[7]]], axis=-1).astype(BF16)
    w_b = jnp.concatenate([w_in[:, :, o[3]:o[4]], w_in[:, :, o[7]:o[9]]], axis=-1).astype(BF16)
    p_ret_b = p_ret.astype(BF16)
    p_sb_b = p_sb.astype(BF16)
    w_out_b = w_out.astype(BF16)
    w_up_b = w_up.astype(BF16)
    w_down_b = w_down.astype(BF16)

    x2 = x.reshape(batch * seq, d)
    for l in range(depth):
        gm = g_mix[l].reshape(1, d)
        pa = _norm_proj(x2, gm, w_a[l], cos_t, sin_t, out_dtype=BF16, rotary=True)
        pb = _norm_proj(x2, gm, w_b[l], cos_t, sin_t, out_dtype=F32, rotary=False)
        ret = _retention(pa, pb, decay, qdec, kdec, chunk_decay, batch=batch, seq=seq)
        sb = _stick_breaking(pa, tri, batch=batch, seq=seq, tb=sb_tb)
        x2 = _merge(ret, sb, pb, x2, p_ret_b[l], p_sb_b[l], w_out_b[l])
        x2 = _mlp(x2, g_mlp[l].reshape(1, d), w_up_b[l], w_down_b[l])
    out = _final_norm(x2, g_final.reshape(1, d))
    return out.reshape(batch, seq, d)
```

```python
import functools

import numpy as np
import jax
import jax.numpy as jnp
from jax import lax
from jax.experimental import pallas as pl
from jax.experimental.pallas import tpu as pltpu

D_MODEL = 1024
RET_HEADS = 4
RET_QK_DIM = 128
RET_V_DIM = 256
RET_CHUNK = 128
SB_HEADS = 8
SB_HEAD_DIM = 128
D_FF = 4 * D_MODEL
ROPE_BASE = 10000.0
NORM_EPS = 1e-6
GN_EPS = 1e-5

RET_QK = RET_HEADS * RET_QK_DIM
RET_V = RET_HEADS * RET_V_DIM
SB_W = SB_HEADS * SB_HEAD_DIM

VMEM_LIMIT_BYTES = 56 * 1024 * 1024

BF16 = jnp.bfloat16
F32 = jnp.float32


def _params(semantics):
    return pltpu.CompilerParams(dimension_semantics=semantics,
                                vmem_limit_bytes=VMEM_LIMIT_BYTES)


def _dot(a, b):
    return jnp.dot(a, b, preferred_element_type=F32)


def _dot_nt(a, b):
    return lax.dot_general(a, b, (((1,), (1,)), ((), ())), preferred_element_type=F32)


def _dot_tn(a, b):
    return lax.dot_general(a, b, (((0,), (0,)), ((), ())), preferred_element_type=F32)


def _rms(xf, gain):
    ms = jnp.mean(xf * xf, axis=-1, keepdims=True)
    return xf * lax.rsqrt(ms + NORM_EPS) * gain


NORM_ROWS = 256


def _norm_to_scratch(x_ref, g_ref, h_ref):
    gain = g_ref[...]

    def rows(c, carry):
        r = pl.ds(pl.multiple_of(c * NORM_ROWS, NORM_ROWS), NORM_ROWS)
        h_ref[r, :] = _rms(x_ref[r, :], gain).astype(BF16)
        return carry

    lax.fori_loop(0, x_ref.shape[0] // NORM_ROWS, rows, 0)


def _norm_proj_kernel(x_ref, g_ref, w_ref, cos_ref, sin_ref, o_ref, h_ref, *, rotary):
    j = pl.program_id(1)

    @pl.when(j == 0)
    def _():
        _norm_to_scratch(x_ref, g_ref, h_ref)

    acc = _dot(h_ref[...], w_ref[...])

    if rotary:
        tn = o_ref.shape[1]
        q_tiles = RET_QK // tn

        @pl.when(j < 2 * q_tiles)
        def _():
            cos = cos_ref[...]
            sin = sin_ref[...]
            scale = jnp.where(j >= q_tiles, RET_QK_DIM ** -0.5, 1.0).astype(F32)
            for g in range(tn // RET_QK_DIM):
                sl = slice(g * RET_QK_DIM, (g + 1) * RET_QK_DIM)
                t = acc[:, sl]
                r = (t * cos + pltpu.roll(t, RET_QK_DIM // 2, axis=1) * sin) * scale
                o_ref[:, sl] = r.astype(o_ref.dtype)

        @pl.when(j >= 2 * q_tiles)
        def _():
            o_ref[...] = acc.astype(o_ref.dtype)
    else:
        o_ref[...] = acc.astype(o_ref.dtype)


def _norm_proj(x2, gain, w, layer, cos, sin, *, out_dtype, rotary, tm=2048, tn=512):
    m, d = x2.shape
    n = w.shape[2]
    seq = cos.shape[0]
    pos_blocks = seq // tm
    return pl.pallas_call(
        functools.partial(_norm_proj_kernel, rotary=rotary),
        out_shape=jax.ShapeDtypeStruct((m, n), out_dtype),
        grid=(m // tm, n // tn),
        in_specs=[
            pl.BlockSpec((tm, d), lambda i, j: (i, 0)),
            pl.BlockSpec((None, 1, d), lambda i, j: (layer, 0, 0)),
            pl.BlockSpec((None, d, tn), lambda i, j: (layer, 0, j)),
            pl.BlockSpec((tm, RET_QK_DIM), lambda i, j: (i % pos_blocks, 0)),
            pl.BlockSpec((tm, RET_QK_DIM), lambda i, j: (i % pos_blocks, 0)),
        ],
        out_specs=pl.BlockSpec((tm, tn), lambda i, j: (i, j)),
        scratch_shapes=[pltpu.VMEM((tm, d), BF16)],
        compiler_params=_params(("parallel", "arbitrary")),
        name="norm_proj_rot" if rotary else "norm_proj",
    )(x2, gain, w, cos, sin)


def _retention_kernel(q_ref, k_ref, v_ref, g_ref, decay_ref, qdec_ref, kdec_ref, o_ref,
                      state_ref, *, chunk_decay):
    n = pl.program_id(1)

    @pl.when(n == 0)
    def _():
        state_ref[...] = jnp.zeros_like(state_ref)

    for h in range(RET_HEADS):
        qk = slice(h * RET_QK_DIM, (h + 1) * RET_QK_DIM)
        vv = slice(h * RET_V_DIM, (h + 1) * RET_V_DIM)
        q = q_ref[:, qk]
        k = k_ref[:, qk]
        v = v_ref[:, vv]
        state = state_ref[h]
        scores = _dot_nt(q, k) * decay_ref[h]
        inner = _dot(scores.astype(BF16), v)
        cross = _dot(q, state.astype(BF16)) * qdec_ref[:, vv]
        kd = (k.astype(F32) * kdec_ref[:, qk]).astype(BF16)
        state_ref[h] = state * chunk_decay[h] + _dot_tn(kd, v)
        y = inner + cross
        mu = jnp.mean(y, axis=-1, keepdims=True)
        yc = y - mu
        var = jnp.mean(yc * yc, axis=-1, keepdims=True)
        yn = yc * lax.rsqrt(var + GN_EPS)
        gate = g_ref[:, vv]
        o_ref[:, vv] = (yn * (gate * jax.nn.sigmoid(gate))).astype(o_ref.dtype)


def _retention(pa, pb, decay, qdec, kdec, chunk_decay, *, batch, seq):
    c = RET_CHUNK
    n_chunks = seq // c
    m = pa.shape[0]
    row = lambda b, n: b * n_chunks + n
    return pl.pallas_call(
        functools.partial(_retention_kernel, chunk_decay=chunk_decay),
        out_shape=jax.ShapeDtypeStruct((m, RET_V), BF16),
        grid=(batch, n_chunks),
        in_specs=[
            pl.BlockSpec((c, RET_QK), lambda b, n: (row(b, n), 0)),
            pl.BlockSpec((c, RET_QK), lambda b, n: (row(b, n), 1)),
            pl.BlockSpec((c, RET_V), lambda b, n: (row(b, n), 1)),
            pl.BlockSpec((c, RET_V), lambda b, n: (row(b, n), 0)),
            pl.BlockSpec((RET_HEADS, c, c), lambda b, n: (0, 0, 0)),
            pl.BlockSpec((c, RET_V), lambda b, n: (0, 0)),
            pl.BlockSpec((c, RET_QK), lambda b, n: (0, 0)),
        ],
        out_specs=pl.BlockSpec((c, RET_V), lambda b, n: (row(b, n), 0)),
        scratch_shapes=[pltpu.VMEM((RET_HEADS, RET_QK_DIM, RET_V_DIM), F32)],
        compiler_params=_params(("parallel", "arbitrary")),
        name="retention",
    )(pa, pa, pa, pb, decay, qdec, kdec)


SB_EXP_UNDERFLOW = -104.0


def _sb_scores(q, kb, tri, mask):
    z = _dot_nt(q, kb) * (SB_HEAD_DIM ** -0.5)
    lse = jnp.log(1.0 + jnp.exp(-jnp.abs(z)))
    log_beta = jnp.minimum(z, 0.0) - lse
    log_1m = log_beta - z
    if mask is not None:
        log_1m = jnp.where(mask, log_1m, 0.0)
    hi = log_1m.astype(BF16)
    lo = (log_1m - hi.astype(F32)).astype(BF16)
    after = _dot(hi, tri) + _dot(lo, tri)
    return log_beta + after, jnp.sum(log_1m, axis=-1, keepdims=True)


def _sb_kernel(q_ref, k_ref, v_ref, tri_ref, o_ref, *, tb, heads):
    ti = pl.program_id(2)
    tri = tri_ref[...]
    rows = lax.broadcasted_iota(jnp.int32, (tb, tb), 0)
    cols = lax.broadcasted_iota(jnp.int32, (tb, tb), 1)
    causal = cols < rows
    start = pl.multiple_of(ti * tb, tb)

    def head(g):
        return slice(g * SB_HEAD_DIM, (g + 1) * SB_HEAD_DIM)

    def diag_tile(g):
        q = q_ref[:, head(g)]
        e, s = _sb_scores(q, k_ref[pl.ds(start, tb), head(g)], tri, causal)
        w = jnp.where(causal, jnp.exp(e), 0.0)
        return q, s, _dot(w.astype(BF16), v_ref[pl.ds(start, tb), head(g)])

    def full_tile(g, q, j, carry, acc):
        off = pl.multiple_of(j * tb, tb)
        e, s = _sb_scores(q, k_ref[pl.ds(off, tb), head(g)], tri, None)
        w = jnp.exp(e + carry)
        return carry + s, acc + _dot(w.astype(BF16), v_ref[pl.ds(off, tb), head(g)])

    @pl.when(ti == 0)
    def _():
        for g in range(heads):
            _, _, acc = diag_tile(g)
            o_ref[:, head(g)] = acc.astype(o_ref.dtype)

    @pl.when(ti > 0)
    def _():
        state = []
        for g in range(heads):
            q, carry, acc = diag_tile(g)
            state.append((q,) + full_tile(g, q, ti - 1, carry, acc))

        for g, (q, carry, acc) in enumerate(state):
            def cond(c):
                return jnp.logical_and(c[0] >= 0, jnp.max(c[1]) > SB_EXP_UNDERFLOW)

            def body(c, g=g, q=q):
                carry, acc = full_tile(g, q, c[0], c[1], c[2])
                return c[0] - 1, carry, acc

            _, _, acc = lax.while_loop(cond, body, (ti - 2, carry, acc))
            o_ref[:, head(g)] = acc.astype(o_ref.dtype)


def _stick_breaking(pa, tri, *, batch, seq, tb=256, heads=4):
    m = pa.shape[0]
    t_blocks = seq // tb
    width = heads * SB_HEAD_DIM
    q_col = (2 * RET_QK + RET_V) // width
    k_col = q_col + SB_W // width
    v_col = k_col + SB_W // width
    return pl.pallas_call(
        functools.partial(_sb_kernel, tb=tb, heads=heads),
        out_shape=jax.ShapeDtypeStruct((m, SB_W), BF16),
        grid=(batch, SB_HEADS // heads, t_blocks),
        in_specs=[
            pl.BlockSpec((tb, width), lambda b, h, t: (b * t_blocks + t, q_col + h)),
            pl.BlockSpec((seq, width), lambda b, h, t: (b, k_col + h)),
            pl.BlockSpec((seq, width), lambda b, h, t: (b, v_col + h)),
            pl.BlockSpec((tb, tb), lambda b, h, t: (0, 0)),
        ],
        out_specs=pl.BlockSpec((tb, width), lambda b, h, t: (b * t_blocks + t, h)),
        compiler_params=_params(("parallel", "parallel", "arbitrary")),
        name="stick_breaking",
    )(pa, pa, pa, tri)


def _merge_kernel(ret_ref, sb_ref, x_ref, g_ref, wg_ref, pr_ref, ps_ref, wo_ref, o_ref):
    d = x_ref.shape[1]
    xf = x_ref[...]
    gates = _dot(_rms(xf, g_ref[...]).astype(BF16), wg_ref[...])
    branch_a = _dot(ret_ref[...], pr_ref[...])
    branch_b = _dot(sb_ref[...], ps_ref[...])
    merged = (jax.nn.sigmoid(gates[:, :d]) * branch_a
              + jax.nn.sigmoid(gates[:, d:]) * branch_b)
    o_ref[...] = xf + _dot(merged.astype(BF16), wo_ref[...])


def _merge(ret, sb, x2, gain, w_gate, p_ret, p_sb, w_out, layer, *, tm=512):
    m, d = x2.shape
    row = lambda i: (i, 0)
    per_layer = lambda i: (layer, 0, 0)
    return pl.pallas_call(
        _merge_kernel,
        out_shape=jax.ShapeDtypeStruct((m, d), F32),
        grid=(m // tm,),
        in_specs=[
            pl.BlockSpec((tm, RET_V), row),
            pl.BlockSpec((tm, SB_W), row),
            pl.BlockSpec((tm, d), row),
            pl.BlockSpec((None, 1, d), per_layer),
            pl.BlockSpec((None, d, 2 * d), per_layer),
            pl.BlockSpec((None, RET_V, d), per_layer),
            pl.BlockSpec((None, SB_W, d), per_layer),
            pl.BlockSpec((None, d, d), per_layer),
        ],
        out_specs=pl.BlockSpec((tm, d), row),
        compiler_params=_params(("parallel",)),
        name="gate_merge",
    )(ret, sb, x2, gain, w_gate, p_ret, p_sb, w_out)


def _mlp_kernel(x_ref, g_ref, wu_ref, wd_ref, o_ref, h_ref):
    j = pl.program_id(1)

    @pl.when(j == 0)
    def _():
        xf = x_ref[...]
        h_ref[...] = _rms(xf, g_ref[...]).astype(BF16)
        o_ref[...] = xf

    up = jnp.maximum(_dot(h_ref[...], wu_ref[...]), 0.0)
    o_ref[...] += _dot((up * up).astype(BF16), wd_ref[...])


def _mlp(x2, gain, w_up, w_down, layer, *, tm=1024, tf=512):
    m, d = x2.shape
    ff = w_up.shape[2]
    return pl.pallas_call(
        _mlp_kernel,
        out_shape=jax.ShapeDtypeStruct((m, d), F32),
        grid=(m // tm, ff // tf),
        in_specs=[
            pl.BlockSpec((tm, d), lambda i, j: (i, 0)),
            pl.BlockSpec((None, 1, d), lambda i, j: (layer, 0, 0)),
            pl.BlockSpec((None, d, tf), lambda i, j: (layer, 0, j)),
            pl.BlockSpec((None, tf, d), lambda i, j: (layer, j, 0)),
        ],
        out_specs=pl.BlockSpec((tm, d), lambda i, j: (i, 0)),
        scratch_shapes=[pltpu.VMEM((tm, d), BF16)],
        compiler_params=_params(("parallel", "arbitrary")),
        name="relu2_mlp",
    )(x2, gain, w_up, w_down)


def _final_norm_kernel(x_ref, g_ref, o_ref):
    o_ref[...] = _rms(x_ref[...], g_ref[...])


def _final_norm(x2, gain, *, tm=1024):
    m, d = x2.shape
    return pl.pallas_call(
        _final_norm_kernel,
        out_shape=jax.ShapeDtypeStruct((m, d), F32),
        grid=(m // tm,),
        in_specs=[pl.BlockSpec((tm, d), lambda i: (i, 0)),
                  pl.BlockSpec((1, d), lambda i: (0, 0))],
        out_specs=pl.BlockSpec((tm, d), lambda i: (i, 0)),
        compiler_params=_params(("parallel",)),
        name="final_norm",
    )(x2, gain)


def _tables(seq, tb):
    half = RET_QK_DIM // 2
    inv_freq = ROPE_BASE ** (-jnp.arange(half, dtype=F32) / half)
    ang = jnp.arange(seq, dtype=F32)[:, None] * inv_freq[None, :]
    cos = jnp.cos(ang)
    sin = jnp.sin(ang)
    cos_t = jnp.concatenate([cos, cos], axis=-1)
    sin_t = jnp.concatenate([-sin, sin], axis=-1)

    log_gamma = jnp.log1p(-(2.0 ** (-5.0 - jnp.arange(RET_HEADS, dtype=F32))))
    c = RET_CHUNK
    idx = jnp.arange(c, dtype=F32)
    diff = idx[:, None] - idx[None, :]
    decay = jnp.where(diff[None] >= 0,
                      jnp.exp(log_gamma[:, None, None] * jnp.maximum(diff, 0.0)[None]), 0.0)
    q_decay = jnp.exp(log_gamma[None, :] * (idx[:, None] + 1.0))
    k_decay = jnp.exp(log_gamma[None, :] * (c - 1.0 - idx[:, None]))
    qdec = jnp.repeat(q_decay, RET_V_DIM, axis=1)
    kdec = jnp.repeat(k_decay, RET_QK_DIM, axis=1)
    gamma = 1.0 - 2.0 ** (-5.0 - np.arange(RET_HEADS, dtype=np.float64))
    chunk_decay = tuple(float(g) for g in gamma ** c)

    s_idx = jnp.arange(tb)
    tri = (s_idx[:, None] > s_idx[None, :]).astype(BF16)
    return cos_t, sin_t, decay, qdec, kdec, chunk_decay, tri


def kernel(x, w_in, p_ret, p_sb, w_out, w_up, w_down, g_mix, g_mlp, g_final):
    batch, seq, d = x.shape
    depth = w_in.shape[0]
    sb_tb = 256
    cos_t, sin_t, decay, qdec, kdec, chunk_decay, tri = _tables(seq, sb_tb)

    o = np.cumsum((0, RET_QK, RET_QK, RET_V, RET_V, SB_W, SB_W, SB_W, D_MODEL, D_MODEL))
    w_a = jnp.concatenate([w_in[:, :, o[0]:o[3]], w_in[:, :, o[4]:o[7]]], axis=-1).astype(BF16)
    w_g = w_in[:, :, o[3]:o[4]].astype(BF16)
    w_gate = w_in[:, :, o[7]:o[9]].astype(BF16)
    p_ret_b = p_ret.astype(BF16)
    p_sb_b = p_sb.astype(BF16)
    w_out_b = w_out.astype(BF16)
    w_up_b = w_up.astype(BF16)
    w_down_b = w_down.astype(BF16)
    g_mix3 = g_mix.reshape(depth, 1, d)
    g_mlp3 = g_mlp.reshape(depth, 1, d)

    x2 = x.reshape(batch * seq, d)
    for l in range(depth):
        pa = _norm_proj(x2, g_mix3, w_a, l, cos_t, sin_t, out_dtype=BF16, rotary=True)
        pb = _norm_proj(x2, g_mix3, w_g, l, cos_t, sin_t, out_dtype=F32, rotary=False)
        ret = _retention(pa, pb, decay, qdec, kdec, chunk_decay, batch=batch, seq=seq)
        sb = _stick_breaking(pa, tri, batch=batch, seq=seq, tb=sb_tb)
        x2 = _merge(ret, sb, x2, g_mix3, w_gate, p_ret_b, p_sb_b, w_out_b, l)
        x2 = _mlp(x2, g_mlp3, w_up_b, w_down_b, l)
    out = _final_norm(x2, g_final.reshape(1, d))
    return out.reshape(batch, seq, d)
```

```python
import functools

import numpy as np
import jax
import jax.numpy as jnp
from jax import lax
from jax.experimental import pallas as pl
from jax.experimental.pallas import tpu as pltpu

D_MODEL = 1024
RET_HEADS = 4
RET_QK_DIM = 128
RET_V_DIM = 256
RET_CHUNK = 128
SB_HEADS = 8
SB_HEAD_DIM = 128
D_FF = 4 * D_MODEL
ROPE_BASE = 10000.0
NORM_EPS = 1e-6
GN_EPS = 1e-5

RET_QK = RET_HEADS * RET_QK_DIM
RET_V = RET_HEADS * RET_V_DIM
SB_W = SB_HEADS * SB_HEAD_DIM

VMEM_LIMIT_BYTES = 56 * 1024 * 1024

BF16 = jnp.bfloat16
F32 = jnp.float32


def _params(semantics):
    return pltpu.CompilerParams(dimension_semantics=semantics,
                                vmem_limit_bytes=VMEM_LIMIT_BYTES)


def _dot(a, b):
    return jnp.dot(a, b, preferred_element_type=F32)


def _dot_nt(a, b):
    return lax.dot_general(a, b, (((1,), (1,)), ((), ())), preferred_element_type=F32)


def _dot_tn(a, b):
    return lax.dot_general(a, b, (((0,), (0,)), ((), ())), preferred_element_type=F32)


def _rms(xf, gain):
    ms = jnp.mean(xf * xf, axis=-1, keepdims=True)
    return xf * lax.rsqrt(ms + NORM_EPS) * gain


NORM_ROWS = 256


def _norm_to_scratch(x_ref, g_ref, h_ref):
    gain = g_ref[...]

    def rows(c, carry):
        r = pl.ds(pl.multiple_of(c * NORM_ROWS, NORM_ROWS), NORM_ROWS)
        h_ref[r, :] = _rms(x_ref[r, :], gain).astype(BF16)
        return carry

    lax.fori_loop(0, x_ref.shape[0] // NORM_ROWS, rows, 0)


def _norm_proj_kernel(x_ref, g_ref, w_ref, cos_ref, sin_ref, o_ref, h_ref):
    j = pl.program_id(1)

    @pl.when(j == 0)
    def _():
        _norm_to_scratch(x_ref, g_ref, h_ref)

    tm, tn = o_ref.shape
    q_tiles = RET_QK // tn
    rotated = j < 2 * q_tiles
    is_key = jnp.logical_and(rotated, j >= q_tiles)
    scale = jnp.where(is_key, RET_QK_DIM ** -0.5, 1.0).astype(F32)
    w = w_ref[...]
    for c in range(tm // NORM_ROWS):
        rows = slice(c * NORM_ROWS, (c + 1) * NORM_ROWS)
        acc = _dot(h_ref[rows, :], w)
        cos = jnp.where(rotated, cos_ref[rows, :], 1.0) * scale
        sin = jnp.where(rotated, sin_ref[rows, :], 0.0) * scale
        for g in range(tn // RET_QK_DIM):
            sl = slice(g * RET_QK_DIM, (g + 1) * RET_QK_DIM)
            t = acc[:, sl]
            r = t * cos + pltpu.roll(t, RET_QK_DIM // 2, axis=1) * sin
            o_ref[rows, sl] = r.astype(o_ref.dtype)


def _norm_proj(x2, gain, w_in, layer, cos, sin, *, tm=2048, tn=512):
    m, d = x2.shape
    seq = cos.shape[0]
    pos_blocks = seq // tm
    front = (2 * RET_QK + RET_V) // tn
    skip = RET_V // tn
    n = 2 * RET_QK + RET_V + 3 * SB_W
    return pl.pallas_call(
        _norm_proj_kernel,
        out_shape=jax.ShapeDtypeStruct((m, n), BF16),
        grid=(m // tm, n // tn),
        in_specs=[
            pl.BlockSpec((tm, d), lambda i, j: (i, 0)),
            pl.BlockSpec((None, 1, d), lambda i, j: (layer, 0, 0)),
            pl.BlockSpec((None, d, tn),
                         lambda i, j: (layer, 0, jnp.where(j < front, j, j + skip))),
            pl.BlockSpec((tm, RET_QK_DIM), lambda i, j: (i % pos_blocks, 0)),
            pl.BlockSpec((tm, RET_QK_DIM), lambda i, j: (i % pos_blocks, 0)),
        ],
        out_specs=pl.BlockSpec((tm, tn), lambda i, j: (i, j)),
        scratch_shapes=[pltpu.VMEM((tm, d), BF16)],
        compiler_params=_params(("parallel", "arbitrary")),
        name="norm_proj",
    )(x2, gain, w_in, cos, sin)


def _retention_kernel(q_ref, k_ref, v_ref, decay_ref, qdec_ref, kdec_ref, o_ref,
                      state_ref, *, chunk_decay, chunks):
    n = pl.program_id(1)

    @pl.when(n == 0)
    def _():
        state_ref[...] = jnp.zeros_like(state_ref)

    for h in range(RET_HEADS):
        qk = slice(h * RET_QK_DIM, (h + 1) * RET_QK_DIM)
        vv = slice(h * RET_V_DIM, (h + 1) * RET_V_DIM)
        state = state_ref[h]
        for c in range(chunks):
            rows = slice(c * RET_CHUNK, (c + 1) * RET_CHUNK)
            q = q_ref[rows, qk]
            k = k_ref[rows, qk]
            v = v_ref[rows, vv]
            scores = _dot_nt(q, k) * decay_ref[h]
            inner = _dot(scores.astype(BF16), v)
            cross = _dot(q, state.astype(BF16)) * qdec_ref[:, vv]
            kd = (k.astype(F32) * kdec_ref[:, qk]).astype(BF16)
            state = state * chunk_decay[h] + _dot_tn(kd, v)
            y = inner + cross
            mu = jnp.mean(y, axis=-1, keepdims=True)
            yc = y - mu
            var = jnp.mean(yc * yc, axis=-1, keepdims=True)
            o_ref[rows, vv] = yc * lax.rsqrt(var + GN_EPS)
        state_ref[h] = state


def _retention(pa, decay, qdec, kdec, chunk_decay, *, batch, seq, chunks=2):
    rows = chunks * RET_CHUNK
    steps = seq // rows
    m = pa.shape[0]
    row = lambda b, n: b * steps + n
    return pl.pallas_call(
        functools.partial(_retention_kernel, chunk_decay=chunk_decay, chunks=chunks),
        out_shape=jax.ShapeDtypeStruct((m, RET_V), F32),
        grid=(batch, steps),
        in_specs=[
            pl.BlockSpec((rows, RET_QK), lambda b, n: (row(b, n), 0)),
            pl.BlockSpec((rows, RET_QK), lambda b, n: (row(b, n), 1)),
            pl.BlockSpec((rows, RET_V), lambda b, n: (row(b, n), 1)),
            pl.BlockSpec((RET_HEADS, RET_CHUNK, RET_CHUNK), lambda b, n: (0, 0, 0)),
            pl.BlockSpec((RET_CHUNK, RET_V), lambda b, n: (0, 0)),
            pl.BlockSpec((RET_CHUNK, RET_QK), lambda b, n: (0, 0)),
        ],
        out_specs=pl.BlockSpec((rows, RET_V), lambda b, n: (row(b, n), 0)),
        scratch_shapes=[pltpu.VMEM((RET_HEADS, RET_QK_DIM, RET_V_DIM), F32)],
        compiler_params=_params(("parallel", "arbitrary")),
        name="retention",
    )(pa, pa, pa, decay, qdec, kdec)


SB_EXP_UNDERFLOW = -104.0


def _sb_scores(q, kb, tri, mask):
    z = _dot_nt(q, kb) * (SB_HEAD_DIM ** -0.5)
    lse = jnp.log(1.0 + jnp.exp(-jnp.abs(z)))
    log_beta = jnp.minimum(z, 0.0) - lse
    log_1m = log_beta - z
    if mask is not None:
        log_1m = jnp.where(mask, log_1m, 0.0)
    hi = log_1m.astype(BF16)
    lo = (log_1m - hi.astype(F32)).astype(BF16)
    after = _dot(hi, tri) + _dot(lo, tri)
    return log_beta + after, jnp.sum(log_1m, axis=-1, keepdims=True)


def _sb_kernel(q_ref, k_ref, v_ref, tri_ref, o_ref, *, tb, heads):
    ti = pl.program_id(2)
    tri = tri_ref[...]
    rows = lax.broadcasted_iota(jnp.int32, (tb, tb), 0)
    cols = lax.broadcasted_iota(jnp.int32, (tb, tb), 1)
    causal = cols < rows
    start = pl.multiple_of(ti * tb, tb)

    def head(g):
        return slice(g * SB_HEAD_DIM, (g + 1) * SB_HEAD_DIM)

    def diag_tile(g):
        q = q_ref[:, head(g)]
        e, s = _sb_scores(q, k_ref[pl.ds(start, tb), head(g)], tri, causal)
        w = jnp.where(causal, jnp.exp(e), 0.0)
        return q, s, _dot(w.astype(BF16), v_ref[pl.ds(start, tb), head(g)])

    def full_tile(g, q, j, carry, acc):
        off = pl.multiple_of(j * tb, tb)
        e, s = _sb_scores(q, k_ref[pl.ds(off, tb), head(g)], tri, None)
        w = jnp.exp(e + carry)
        return carry + s, acc + _dot(w.astype(BF16), v_ref[pl.ds(off, tb), head(g)])

    @pl.when(ti == 0)
    def _():
        for g in range(heads):
            _, _, acc = diag_tile(g)
            o_ref[:, head(g)] = acc.astype(o_ref.dtype)

    @pl.when(ti > 0)
    def _():
        state = []
        for g in range(heads):
            q, carry, acc = diag_tile(g)
            state.append((q,) + full_tile(g, q, ti - 1, carry, acc))

        for g, (q, carry, acc) in enumerate(state):
            def cond(c):
                return jnp.logical_and(c[0] >= 0, jnp.max(c[1]) > SB_EXP_UNDERFLOW)

            def body(c, g=g, q=q):
                carry, acc = full_tile(g, q, c[0], c[1], c[2])
                return c[0] - 1, carry, acc

            _, _, acc = lax.while_loop(cond, body, (ti - 2, carry, acc))
            o_ref[:, head(g)] = acc.astype(o_ref.dtype)


def _stick_breaking(pa, tri, *, batch, seq, tb=256, heads=4):
    m = pa.shape[0]
    t_blocks = seq // tb
    width = heads * SB_HEAD_DIM
    q_col = (2 * RET_QK + RET_V) // width
    k_col = q_col + SB_W // width
    v_col = k_col + SB_W // width
    return pl.pallas_call(
        functools.partial(_sb_kernel, tb=tb, heads=heads),
        out_shape=jax.ShapeDtypeStruct((m, SB_W), BF16),
        grid=(batch, SB_HEADS // heads, t_blocks),
        in_specs=[
            pl.BlockSpec((tb, width), lambda b, h, t: (b * t_blocks + t, q_col + h)),
            pl.BlockSpec((seq, width), lambda b, h, t: (b, k_col + h)),
            pl.BlockSpec((seq, width), lambda b, h, t: (b, v_col + h)),
            pl.BlockSpec((tb, tb), lambda b, h, t: (0, 0)),
        ],
        out_specs=pl.BlockSpec((tb, width), lambda b, h, t: (b * t_blocks + t, h)),
        compiler_params=_params(("parallel", "parallel", "arbitrary")),
        name="stick_breaking",
    )(pa, pa, pa, tri)


def _merge_kernel(ret_ref, sb_ref, x_ref, g_ref, wsw_ref, wg_ref, pr_ref, ps_ref, wo_ref, o_ref):
    d = x_ref.shape[1]
    xf = x_ref[...]
    h = _rms(xf, g_ref[...]).astype(BF16)
    swish = _dot(h, wsw_ref[...])
    gates = _dot(h, wg_ref[...])
    gated_ret = ret_ref[...] * (swish * jax.nn.sigmoid(swish))
    branch_a = _dot(gated_ret.astype(BF16), pr_ref[...])
    branch_b = _dot(sb_ref[...], ps_ref[...])
    merged = (jax.nn.sigmoid(gates[:, :d]) * branch_a
              + jax.nn.sigmoid(gates[:, d:]) * branch_b)
    o_ref[...] = xf + _dot(merged.astype(BF16), wo_ref[...])


def _merge(ret, sb, x2, gain, w_in, p_ret, p_sb, w_out, layer, *, tm=512):
    m, d = x2.shape
    row = lambda i: (i, 0)
    per_layer = lambda i: (layer, 0, 0)
    swish_col = (2 * RET_QK + RET_V) // RET_V
    gate_col = (2 * RET_QK + 2 * RET_V + 3 * SB_W) // (2 * d)

    def weight(shape, index_map):
        return pl.BlockSpec(shape, index_map, pipeline_mode=pl.Buffered(1))

    return pl.pallas_call(
        _merge_kernel,
        out_shape=jax.ShapeDtypeStruct((m, d), F32),
        grid=(m // tm,),
        in_specs=[
            pl.BlockSpec((tm, RET_V), row),
            pl.BlockSpec((tm, SB_W), row),
            pl.BlockSpec((tm, d), row),
            pl.BlockSpec((None, 1, d), per_layer),
            weight((None, d, RET_V), lambda i: (layer, 0, swish_col)),
            weight((None, d, 2 * d), lambda i: (layer, 0, gate_col)),
            weight((None, RET_V, d), per_layer),
            weight((None, SB_W, d), per_layer),
            weight((None, d, d), per_layer),
        ],
        out_specs=pl.BlockSpec((tm, d), row),
        compiler_params=_params(("parallel",)),
        name="gate_merge",
    )(ret, sb, x2, gain, w_in, w_in, p_ret, p_sb, w_out)


def _mlp_kernel(x_ref, g_ref, wu_ref, wd_ref, gf_ref, o_ref, h_ref, *, final_norm):
    j = pl.program_id(1)

    @pl.when(j == 0)
    def _():
        _norm_to_scratch(x_ref, g_ref, h_ref)
        o_ref[...] = x_ref[...]

    up = jnp.maximum(_dot(h_ref[...], wu_ref[...]), 0.0)
    o_ref[...] += _dot((up * up).astype(BF16), wd_ref[...])

    if final_norm:
        @pl.when(j == pl.num_programs(1) - 1)
        def _():
            gain = gf_ref[...]

            def rows(c, carry):
                r = pl.ds(pl.multiple_of(c * NORM_ROWS, NORM_ROWS), NORM_ROWS)
                o_ref[r, :] = _rms(o_ref[r, :], gain)
                return carry

            lax.fori_loop(0, o_ref.shape[0] // NORM_ROWS, rows, 0)


def _mlp(x2, gain, w_up, w_down, g_final, layer, *, final_norm, tm=1024, tf=512):
    m, d = x2.shape
    ff = w_up.shape[2]
    return pl.pallas_call(
        functools.partial(_mlp_kernel, final_norm=final_norm),
        out_shape=jax.ShapeDtypeStruct((m, d), F32),
        grid=(m // tm, ff // tf),
        in_specs=[
            pl.BlockSpec((tm, d), lambda i, j: (i, 0)),
            pl.BlockSpec((None, 1, d), lambda i, j: (layer, 0, 0)),
            pl.BlockSpec((None, d, tf), lambda i, j: (layer, 0, j)),
            pl.BlockSpec((None, tf, d), lambda i, j: (layer, j, 0)),
            pl.BlockSpec((1, d), lambda i, j: (0, 0)),
        ],
        out_specs=pl.BlockSpec((tm, d), lambda i, j: (i, 0)),
        scratch_shapes=[pltpu.VMEM((tm, d), BF16)],
        compiler_params=_params(("parallel", "arbitrary")),
        name="relu2_mlp_final" if final_norm else "relu2_mlp",
    )(x2, gain, w_up, w_down, g_final)


def _tables(seq, tb):
    half = RET_QK_DIM // 2
    inv_freq = ROPE_BASE ** (-jnp.arange(half, dtype=F32) / half)
    ang = jnp.arange(seq, dtype=F32)[:, None] * inv_freq[None, :]
    cos = jnp.cos(ang)
    sin = jnp.sin(ang)
    cos_t = jnp.concatenate([cos, cos], axis=-1)
    sin_t = jnp.concatenate([-sin, sin], axis=-1)

    log_gamma = jnp.log1p(-(2.0 ** (-5.0 - jnp.arange(RET_HEADS, dtype=F32))))
    c = RET_CHUNK
    idx = jnp.arange(c, dtype=F32)
    diff = idx[:, None] - idx[None, :]
    decay = jnp.where(diff[None] >= 0,
                      jnp.exp(log_gamma[:, None, None] * jnp.maximum(diff, 0.0)[None]), 0.0)
    q_decay = jnp.exp(log_gamma[None, :] * (idx[:, None] + 1.0))
    k_decay = jnp.exp(log_gamma[None, :] * (c - 1.0 - idx[:, None]))
    qdec = jnp.repeat(q_decay, RET_V_DIM, axis=1)
    kdec = jnp.repeat(k_decay, RET_QK_DIM, axis=1)
    gamma = 1.0 - 2.0 ** (-5.0 - np.arange(RET_HEADS, dtype=np.float64))
    chunk_decay = tuple(float(g) for g in gamma ** c)

    s_idx = jnp.arange(tb)
    tri = (s_idx[:, None] > s_idx[None, :]).astype(BF16)
    return cos_t, sin_t, decay, qdec, kdec, chunk_decay, tri


def kernel(x, w_in, p_ret, p_sb, w_out, w_up, w_down, g_mix, g_mlp, g_final):
    batch, seq, d = x.shape
    depth = w_in.shape[0]
    sb_tb = 256
    cos_t, sin_t, decay, qdec, kdec, chunk_decay, tri = _tables(seq, sb_tb)

    w_in_b = w_in.astype(BF16)
    p_ret_b = p_ret.astype(BF16)
    p_sb_b = p_sb.astype(BF16)
    w_out_b = w_out.astype(BF16)
    w_up_b = w_up.astype(BF16)
    w_down_b = w_down.astype(BF16)
    g_mix3 = g_mix.reshape(depth, 1, d)
    g_mlp3 = g_mlp.reshape(depth, 1, d)

    x2 = x.reshape(batch * seq, d)
    g_fin = g_final.reshape(1, d)
    for l in range(depth):
        pa = _norm_proj(x2, g_mix3, w_in_b, l, cos_t, sin_t)
        ret = _retention(pa, decay, qdec, kdec, chunk_decay, batch=batch, seq=seq)
        sb = _stick_breaking(pa, tri, batch=batch, seq=seq, tb=sb_tb)
        x2 = _merge(ret, sb, x2, g_mix3, w_in_b, p_ret_b, p_sb_b, w_out_b, l)
        x2 = _mlp(x2, g_mlp3, w_up_b, w_down_b, g_fin, l, final_norm=(l == depth - 1))
    return x2.reshape(batch, seq, d)
```

```python
import functools

import numpy as np
import jax
import jax.numpy as jnp
from jax import lax
from jax.experimental import pallas as pl
from jax.experimental.pallas import tpu as pltpu

D_MODEL = 1024
RET_HEADS = 4
RET_QK_DIM = 128
RET_V_DIM = 256
RET_CHUNK = 128
SB_HEADS = 8
SB_HEAD_DIM = 128
D_FF = 4 * D_MODEL
ROPE_BASE = 10000.0
NORM_EPS = 1e-6
GN_EPS = 1e-5

RET_QK = RET_HEADS * RET_QK_DIM
RET_V = RET_HEADS * RET_V_DIM
SB_W = SB_HEADS * SB_HEAD_DIM

VMEM_LIMIT_BYTES = 56 * 1024 * 1024

BF16 = jnp.bfloat16
F32 = jnp.float32


def _params(semantics):
    return pltpu.CompilerParams(dimension_semantics=semantics,
                                vmem_limit_bytes=VMEM_LIMIT_BYTES)


def _dot(a, b):
    return jnp.dot(a, b, preferred_element_type=F32)


def _dot_nt(a, b):
    return lax.dot_general(a, b, (((1,), (1,)), ((), ())), preferred_element_type=F32)


def _dot_tn(a, b):
    return lax.dot_general(a, b, (((0,), (0,)), ((), ())), preferred_element_type=F32)


def _rms(xf, gain):
    ms = jnp.mean(xf * xf, axis=-1, keepdims=True)
    return xf * lax.rsqrt(ms + NORM_EPS) * gain


NORM_ROWS = 256


def _norm_to_scratch(x_ref, g_ref, h_ref):
    gain = g_ref[...]

    def rows(c, carry):
        r = pl.ds(pl.multiple_of(c * NORM_ROWS, NORM_ROWS), NORM_ROWS)
        h_ref[r, :] = _rms(x_ref[r, :], gain).astype(BF16)
        return carry

    lax.fori_loop(0, x_ref.shape[0] // NORM_ROWS, rows, 0)


def _norm_proj_kernel(x_ref, g_ref, w_ref, cos_ref, sin_ref, o_ref, h_ref):
    j = pl.program_id(1)

    @pl.when(j == 0)
    def _():
        _norm_to_scratch(x_ref, g_ref, h_ref)

    tm, tn = o_ref.shape
    q_tiles = RET_QK // tn
    rotated = j < 2 * q_tiles
    is_key = jnp.logical_and(rotated, j >= q_tiles)
    scale = jnp.where(is_key, RET_QK_DIM ** -0.5, 1.0).astype(F32)
    w = w_ref[...]
    for c in range(tm // NORM_ROWS):
        rows = slice(c * NORM_ROWS, (c + 1) * NORM_ROWS)
        acc = _dot(h_ref[rows, :], w)
        cos = jnp.where(rotated, cos_ref[rows, :], 1.0) * scale
        sin = jnp.where(rotated, sin_ref[rows, :], 0.0) * scale
        for g in range(tn // RET_QK_DIM):
            sl = slice(g * RET_QK_DIM, (g + 1) * RET_QK_DIM)
            t = acc[:, sl]
            r = t * cos + pltpu.roll(t, RET_QK_DIM // 2, axis=1) * sin
            o_ref[rows, sl] = r.astype(o_ref.dtype)


def _norm_proj(x2, gain, w_in, layer, cos, sin, *, tm=2048, tn=512):
    m, d = x2.shape
    seq = cos.shape[0]
    pos_blocks = seq // tm
    front = (2 * RET_QK + RET_V) // tn
    skip = RET_V // tn
    n = 2 * RET_QK + RET_V + 3 * SB_W
    return pl.pallas_call(
        _norm_proj_kernel,
        out_shape=jax.ShapeDtypeStruct((m, n), BF16),
        grid=(m // tm, n // tn),
        in_specs=[
            pl.BlockSpec((tm, d), lambda i, j: (i, 0)),
            pl.BlockSpec((None, 1, d), lambda i, j: (layer, 0, 0)),
            pl.BlockSpec((None, d, tn),
                         lambda i, j: (layer, 0, jnp.where(j < front, j, j + skip))),
            pl.BlockSpec((tm, RET_QK_DIM), lambda i, j: (i % pos_blocks, 0)),
            pl.BlockSpec((tm, RET_QK_DIM), lambda i, j: (i % pos_blocks, 0)),
        ],
        out_specs=pl.BlockSpec((tm, tn), lambda i, j: (i, j)),
        scratch_shapes=[pltpu.VMEM((tm, d), BF16)],
        compiler_params=_params(("parallel", "arbitrary")),
        name="norm_proj",
    )(x2, gain, w_in, cos, sin)


def _retention_kernel(q_ref, k_ref, v_ref, decay_ref, qdec_ref, kdec_ref, o_ref,
                      state_ref, *, chunk_decay, chunks):
    n = pl.program_id(1)

    @pl.when(n == 0)
    def _():
        state_ref[...] = jnp.zeros_like(state_ref)

    states = [state_ref[h] for h in range(RET_HEADS)]
    items = [(c, h) for c in range(chunks) for h in range(RET_HEADS)]
    n = len(items)

    def slices(item):
        c, h = item
        return (slice(c * RET_CHUNK, (c + 1) * RET_CHUNK),
                slice(h * RET_QK_DIM, (h + 1) * RET_QK_DIM),
                slice(h * RET_V_DIM, (h + 1) * RET_V_DIM))

    first, second = {}, {}
    for i in range(n + 2):
        if i < n:
            rows, qk, vv = slices(items[i])
            h = items[i][1]
            q = q_ref[rows, qk]
            k = k_ref[rows, qk]
            v = v_ref[rows, vv]
            scores = _dot_nt(q, k) * decay_ref[h]
            cross = _dot(q, states[h].astype(BF16)) * qdec_ref[:, vv]
            kd = (k.astype(F32) * kdec_ref[:, qk]).astype(BF16)
            states[h] = states[h] * chunk_decay[h] + _dot_tn(kd, v)
            first[i] = (scores.astype(BF16), cross)
        if 0 <= i - 1 < n:
            rows, qk, vv = slices(items[i - 1])
            scores, cross = first.pop(i - 1)
            second[i - 1] = _dot(scores, v_ref[rows, vv]) + cross
        if 0 <= i - 2 < n:
            rows, qk, vv = slices(items[i - 2])
            y = second.pop(i - 2)
            mu = jnp.mean(y, axis=-1, keepdims=True)
            yc = y - mu
            var = jnp.mean(yc * yc, axis=-1, keepdims=True)
            o_ref[rows, vv] = yc * lax.rsqrt(var + GN_EPS)
    for h in range(RET_HEADS):
        state_ref[h] = states[h]


def _retention(pa, decay, qdec, kdec, chunk_decay, *, batch, seq, chunks=4):
    rows = chunks * RET_CHUNK
    steps = seq // rows
    m = pa.shape[0]
    row = lambda b, n: b * steps + n
    return pl.pallas_call(
        functools.partial(_retention_kernel, chunk_decay=chunk_decay, chunks=chunks),
        out_shape=jax.ShapeDtypeStruct((m, RET_V), F32),
        grid=(batch, steps),
        in_specs=[
            pl.BlockSpec((rows, RET_QK), lambda b, n: (row(b, n), 0)),
            pl.BlockSpec((rows, RET_QK), lambda b, n: (row(b, n), 1)),
            pl.BlockSpec((rows, RET_V), lambda b, n: (row(b, n), 1)),
            pl.BlockSpec((RET_HEADS, RET_CHUNK, RET_CHUNK), lambda b, n: (0, 0, 0)),
            pl.BlockSpec((RET_CHUNK, RET_V), lambda b, n: (0, 0)),
            pl.BlockSpec((RET_CHUNK, RET_QK), lambda b, n: (0, 0)),
        ],
        out_specs=pl.BlockSpec((rows, RET_V), lambda b, n: (row(b, n), 0)),
        scratch_shapes=[pltpu.VMEM((RET_HEADS, RET_QK_DIM, RET_V_DIM), F32)],
        compiler_params=_params(("parallel", "arbitrary")),
        name="retention",
    )(pa, pa, pa, decay, qdec, kdec)


SB_EXP_UNDERFLOW = -104.0
SKEW_A = 2
SKEW_B = 4


def _sb_scores(q, kb, tri, mask):
    z = _dot_nt(q, kb) * (SB_HEAD_DIM ** -0.5)
    lse = jnp.log(1.0 + jnp.exp(-jnp.abs(z)))
    log_beta = jnp.minimum(z, 0.0) - lse
    log_1m = log_beta - z
    if mask is not None:
        log_1m = jnp.where(mask, log_1m, 0.0)
    hi = log_1m.astype(BF16)
    lo = (log_1m - hi.astype(F32)).astype(BF16)
    after = _dot(hi, tri) + _dot(lo, tri)
    return log_beta + after, jnp.sum(log_1m, axis=-1, keepdims=True)


def _sb_kernel(q_ref, k_ref, v_ref, tri_ref, o_ref, *, tb, heads):
    ti = pl.program_id(2)
    tri = tri_ref[...]
    rows = lax.broadcasted_iota(jnp.int32, (tb, tb), 0)
    cols = lax.broadcasted_iota(jnp.int32, (tb, tb), 1)
    causal = cols < rows
    start = pl.multiple_of(ti * tb, tb)

    def head(g):
        return slice(g * SB_HEAD_DIM, (g + 1) * SB_HEAD_DIM)

    def diag_tile(g):
        q = q_ref[:, head(g)]
        e, s = _sb_scores(q, k_ref[pl.ds(start, tb), head(g)], tri, causal)
        w = jnp.where(causal, jnp.exp(e), 0.0)
        return q, s, _dot(w.astype(BF16), v_ref[pl.ds(start, tb), head(g)])

    def full_tile(g, q, j, carry, acc):
        off = pl.multiple_of(j * tb, tb)
        e, s = _sb_scores(q, k_ref[pl.ds(off, tb), head(g)], tri, None)
        w = jnp.exp(e + carry)
        return carry + s, acc + _dot(w.astype(BF16), v_ref[pl.ds(off, tb), head(g)])

    def static_tiles(previous):
        tiles = [(g, kind) for g in range(heads) for kind in (("D", "P") if previous else ("D",))]
        n = len(tiles)
        offset = {"D": start, "P": pl.multiple_of((ti - 1) * tb, tb)}
        z, surv, expo, carry, acc = {}, {}, {}, {}, {}
        for i in range(n + SKEW_B):
            if i < n:
                g, kind = tiles[i]
                z[i] = _dot_nt(q_ref[:, head(g)],
                               k_ref[pl.ds(offset[kind], tb), head(g)]) * (SB_HEAD_DIM ** -0.5)
            if 0 <= i - SKEW_A < n:
                zt = z.pop(i - SKEW_A)
                lse = jnp.log(1.0 + jnp.exp(-jnp.abs(zt)))
                log_beta = jnp.minimum(zt, 0.0) - lse
                log_1m = log_beta - zt
                if tiles[i - SKEW_A][1] == "D":
                    log_1m = jnp.where(causal, log_1m, 0.0)
                hi = log_1m.astype(BF16)
                lo = (log_1m - hi.astype(F32)).astype(BF16)
                s = jnp.sum(log_1m, axis=-1, keepdims=True)
                expo[i - SKEW_A] = (log_beta + (_dot(hi, tri) + _dot(lo, tri)), s)
            if 0 <= i - SKEW_B < n:
                g, kind = tiles[i - SKEW_B]
                e, s = expo.pop(i - SKEW_B)
                vb = v_ref[pl.ds(offset[kind], tb), head(g)]
                if kind == "D":
                    w = jnp.where(causal, jnp.exp(e), 0.0)
                    acc[g] = _dot(w.astype(BF16), vb)
                    carry[g] = s
                else:
                    w = jnp.exp(e + carry[g])
                    acc[g] = acc[g] + _dot(w.astype(BF16), vb)
                    carry[g] = carry[g] + s
        return [(carry[g], acc[g]) for g in range(heads)]

    @pl.when(ti == 0)
    def _():
        for g, (_, acc) in enumerate(static_tiles(previous=False)):
            o_ref[:, head(g)] = acc.astype(o_ref.dtype)

    @pl.when(ti > 0)
    def _():
        state = [(q_ref[:, head(g)], carry, acc)
                 for g, (carry, acc) in enumerate(static_tiles(previous=True))]

        for g, (q, carry, acc) in enumerate(state):
            def cond(c):
                return jnp.logical_and(c[0] >= 0, jnp.max(c[1]) > SB_EXP_UNDERFLOW)

            def body(c, g=g, q=q):
                carry, acc = full_tile(g, q, c[0], c[1], c[2])
                return c[0] - 1, carry, acc

            _, _, acc = lax.while_loop(cond, body, (ti - 2, carry, acc))
            o_ref[:, head(g)] = acc.astype(o_ref.dtype)


def _stick_breaking(pa, tri, *, batch, seq, tb=256, heads=8):
    m = pa.shape[0]
    t_blocks = seq // tb
    width = heads * SB_HEAD_DIM
    q_col = (2 * RET_QK + RET_V) // width
    k_col = q_col + SB_W // width
    v_col = k_col + SB_W // width
    return pl.pallas_call(
        functools.partial(_sb_kernel, tb=tb, heads=heads),
        out_shape=jax.ShapeDtypeStruct((m, SB_W), BF16),
        grid=(batch, SB_HEADS // heads, t_blocks),
        in_specs=[
            pl.BlockSpec((tb, width), lambda b, h, t: (b * t_blocks + t, q_col + h)),
            pl.BlockSpec((seq, width), lambda b, h, t: (b, k_col + h)),
            pl.BlockSpec((seq, width), lambda b, h, t: (b, v_col + h)),
            pl.BlockSpec((tb, tb), lambda b, h, t: (0, 0)),
        ],
        out_specs=pl.BlockSpec((tb, width), lambda b, h, t: (b * t_blocks + t, h)),
        compiler_params=_params(("parallel", "parallel", "arbitrary")),
        name="stick_breaking",
    )(pa, pa, pa, tri)


def _merge_kernel(ret_ref, sb_ref, x_ref, g_ref, wsw_ref, wg_ref, pr_ref, ps_ref, wo_ref, o_ref):
    d = x_ref.shape[1]
    xf = x_ref[...]
    h = _rms(xf, g_ref[...]).astype(BF16)
    swish = _dot(h, wsw_ref[...])
    gates = _dot(h, wg_ref[...])
    gated_ret = ret_ref[...] * (swish * jax.nn.sigmoid(swish))
    branch_a = _dot(gated_ret.astype(BF16), pr_ref[...])
    branch_b = _dot(sb_ref[...], ps_ref[...])
    merged = (jax.nn.sigmoid(gates[:, :d]) * branch_a
              + jax.nn.sigmoid(gates[:, d:]) * branch_b)
    o_ref[...] = xf + _dot(merged.astype(BF16), wo_ref[...])


def _merge(ret, sb, x2, gain, w_in, p_ret, p_sb, w_out, layer, *, tm=512):
    m, d = x2.shape
    row = lambda i: (i, 0)
    per_layer = lambda i: (layer, 0, 0)
    swish_col = (2 * RET_QK + RET_V) // RET_V
    gate_col = (2 * RET_QK + 2 * RET_V + 3 * SB_W) // (2 * d)

    def weight(shape, index_map):
        return pl.BlockSpec(shape, index_map, pipeline_mode=pl.Buffered(1))

    return pl.pallas_call(
        _merge_kernel,
        out_shape=jax.ShapeDtypeStruct((m, d), F32),
        grid=(m // tm,),
        in_specs=[
            pl.BlockSpec((tm, RET_V), row),
            pl.BlockSpec((tm, SB_W), row),
            pl.BlockSpec((tm, d), row),
            pl.BlockSpec((None, 1, d), per_layer),
            weight((None, d, RET_V), lambda i: (layer, 0, swish_col)),
            weight((None, d, 2 * d), lambda i: (layer, 0, gate_col)),
            weight((None, RET_V, d), per_layer),
            weight((None, SB_W, d), per_layer),
            weight((None, d, d), per_layer),
        ],
        out_specs=pl.BlockSpec((tm, d), row),
        compiler_params=_params(("parallel",)),
        name="gate_merge",
    )(ret, sb, x2, gain, w_in, w_in, p_ret, p_sb, w_out)


def _mlp_kernel(x_ref, g_ref, wu_ref, wd_ref, gf_ref, o_ref, h_ref, *, final_norm):
    j = pl.program_id(1)

    @pl.when(j == 0)
    def _():
        xf = x_ref[...]
        h_ref[...] = _rms(xf, g_ref[...]).astype(BF16)
        o_ref[...] = xf

    up = jnp.maximum(_dot(h_ref[...], wu_ref[...]), 0.0)
    o_ref[...] += _dot((up * up).astype(BF16), wd_ref[...])

    if final_norm:
        @pl.when(j == pl.num_programs(1) - 1)
        def _():
            gain = gf_ref[...]

            def rows(c, carry):
                r = pl.ds(pl.multiple_of(c * NORM_ROWS, NORM_ROWS), NORM_ROWS)
                o_ref[r, :] = _rms(o_ref[r, :], gain)
                return carry

            lax.fori_loop(0, o_ref.shape[0] // NORM_ROWS, rows, 0)


def _mlp(x2, gain, w_up, w_down, g_final, layer, *, final_norm, tm=1024, tf=1024):
    m, d = x2.shape
    ff = w_up.shape[2]
    return pl.pallas_call(
        functools.partial(_mlp_kernel, final_norm=final_norm),
        out_shape=jax.ShapeDtypeStruct((m, d), F32),
        grid=(m // tm, ff // tf),
        in_specs=[
            pl.BlockSpec((tm, d), lambda i, j: (i, 0)),
            pl.BlockSpec((None, 1, d), lambda i, j: (layer, 0, 0)),
            pl.BlockSpec((None, d, tf), lambda i, j: (layer, 0, j)),
            pl.BlockSpec((None, tf, d), lambda i, j: (layer, j, 0)),
            pl.BlockSpec((1, d), lambda i, j: (0, 0)),
        ],
        out_specs=pl.BlockSpec((tm, d), lambda i, j: (i, 0)),
        scratch_shapes=[pltpu.VMEM((tm, d), BF16)],
        compiler_params=_params(("parallel", "arbitrary")),
        name="relu2_mlp_final" if final_norm else "relu2_mlp",
    )(x2, gain, w_up, w_down, g_final)


def _tables(seq, tb):
    half = RET_QK_DIM // 2
    inv_freq = ROPE_BASE ** (-jnp.arange(half, dtype=F32) / half)
    ang = jnp.arange(seq, dtype=F32)[:, None] * inv_freq[None, :]
    cos = jnp.cos(ang)
    sin = jnp.sin(ang)
    cos_t = jnp.concatenate([cos, cos], axis=-1)
    sin_t = jnp.concatenate([-sin, sin], axis=-1)

    log_gamma = jnp.log1p(-(2.0 ** (-5.0 - jnp.arange(RET_HEADS, dtype=F32))))
    c = RET_CHUNK
    idx = jnp.arange(c, dtype=F32)
    diff = idx[:, None] - idx[None, :]
    decay = jnp.where(diff[None] >= 0,
                      jnp.exp(log_gamma[:, None, None] * jnp.maximum(diff, 0.0)[None]), 0.0)
    q_decay = jnp.exp(log_gamma[None, :] * (idx[:, None] + 1.0))
    k_decay = jnp.exp(log_gamma[None, :] * (c - 1.0 - idx[:, None]))
    qdec = jnp.repeat(q_decay, RET_V_DIM, axis=1)
    kdec = jnp.repeat(k_decay, RET_QK_DIM, axis=1)
    gamma = 1.0 - 2.0 ** (-5.0 - np.arange(RET_HEADS, dtype=np.float64))
    chunk_decay = tuple(float(g) for g in gamma ** c)

    s_idx = jnp.arange(tb)
    tri = (s_idx[:, None] > s_idx[None, :]).astype(BF16)
    return cos_t, sin_t, decay, qdec, kdec, chunk_decay, tri


def kernel(x, w_in, p_ret, p_sb, w_out, w_up, w_down, g_mix, g_mlp, g_final):
    batch, seq, d = x.shape
    depth = w_in.shape[0]
    sb_tb = 256
    cos_t, sin_t, decay, qdec, kdec, chunk_decay, tri = _tables(seq, sb_tb)

    w_in_b = w_in.astype(BF16)
    p_ret_b = p_ret.astype(BF16)
    p_sb_b = p_sb.astype(BF16)
    w_out_b = w_out.astype(BF16)
    w_up_b = w_up.astype(BF16)
    w_down_b = w_down.astype(BF16)
    g_mix3 = g_mix.reshape(depth, 1, d)
    g_mlp3 = g_mlp.reshape(depth, 1, d)

    x2 = x.reshape(batch * seq, d)
    g_fin = g_final.reshape(1, d)
    for l in range(depth):
        pa = _norm_proj(x2, g_mix3, w_in_b, l, cos_t, sin_t)
        ret = _retention(pa, decay, qdec, kdec, chunk_decay, batch=batch, seq=seq)
        sb = _stick_breaking(pa, tri, batch=batch, seq=seq, tb=sb_tb)
        x2 = _merge(ret, sb, x2, g_mix3, w_in_b, p_ret_b, p_sb_b, w_out_b, l)
        x2 = _mlp(x2, g_mlp3, w_up_b, w_down_b, g_fin, l, final_norm=(l == depth - 1))
    return x2.reshape(batch, seq, d)
```

```python
import functools

import numpy as np
import jax
import jax.numpy as jnp
from jax import lax
from jax.experimental import pallas as pl
from jax.experimental.pallas import tpu as pltpu

D_MODEL = 1024
RET_HEADS = 4
RET_QK_DIM = 128
RET_V_DIM = 256
RET_CHUNK = 128
SB_HEADS = 8
SB_HEAD_DIM = 128
D_FF = 4 * D_MODEL
ROPE_BASE = 10000.0
NORM_EPS = 1e-6
GN_EPS = 1e-5

RET_QK = RET_HEADS * RET_QK_DIM
RET_V = RET_HEADS * RET_V_DIM
SB_W = SB_HEADS * SB_HEAD_DIM

VMEM_LIMIT_BYTES = 56 * 1024 * 1024

BF16 = jnp.bfloat16
F32 = jnp.float32


def _params(semantics):
    return pltpu.CompilerParams(dimension_semantics=semantics,
                                vmem_limit_bytes=VMEM_LIMIT_BYTES)


def _dot(a, b):
    return jnp.dot(a, b, preferred_element_type=F32)


def _dot_nt(a, b):
    return lax.dot_general(a, b, (((1,), (1,)), ((), ())), preferred_element_type=F32)


def _dot_tn(a, b):
    return lax.dot_general(a, b, (((0,), (0,)), ((), ())), preferred_element_type=F32)


def _rms(xf, gain):
    ms = jnp.mean(xf * xf, axis=-1, keepdims=True)
    return xf * lax.rsqrt(ms + NORM_EPS) * gain


NORM_ROWS = 256
PROJ_ROWS = 256


def _norm_to_scratch(x_ref, g_ref, h_ref):
    gain = g_ref[...]

    def rows(c, carry):
        r = pl.ds(pl.multiple_of(c * NORM_ROWS, NORM_ROWS), NORM_ROWS)
        h_ref[r, :] = _rms(x_ref[r, :], gain).astype(BF16)
        return carry

    lax.fori_loop(0, x_ref.shape[0] // NORM_ROWS, rows, 0)


def _norm_proj_kernel(x_ref, g_ref, w_ref, cos_ref, sin_ref, o_ref, h_ref):
    j = pl.program_id(1)

    @pl.when(j == 0)
    def _():
        _norm_to_scratch(x_ref, g_ref, h_ref)

    tm, tn = o_ref.shape
    rotated = j == 0
    key_scale = jnp.where(rotated, RET_QK_DIM ** -0.5, 1.0).astype(F32)
    w = w_ref[...]
    for c in range(tm // PROJ_ROWS):
        rows = slice(c * PROJ_ROWS, (c + 1) * PROJ_ROWS)
        acc = _dot(h_ref[rows, :], w)
        cos = jnp.where(rotated, cos_ref[rows, :], 1.0)
        sin = jnp.where(rotated, sin_ref[rows, :], 0.0)
        cos_k = cos * key_scale
        sin_k = sin * key_scale
        for g in range(tn // RET_QK_DIM):
            sl = slice(g * RET_QK_DIM, (g + 1) * RET_QK_DIM)
            t = acc[:, sl]
            if g < RET_HEADS:
                r = t * cos + pltpu.roll(t, RET_QK_DIM // 2, axis=1) * sin
            else:
                r = t * cos_k + pltpu.roll(t, RET_QK_DIM // 2, axis=1) * sin_k
            o_ref[rows, sl] = r.astype(o_ref.dtype)


def _norm_proj(x2, gain, w_in, layer, cos, sin, *, tm=2048, tn=2 * RET_QK):
    m, d = x2.shape
    seq = cos.shape[0]
    pos_blocks = seq // tm
    front = (2 * RET_QK + RET_V) // tn
    skip = RET_V // tn
    n = 2 * RET_QK + RET_V + 3 * SB_W
    return pl.pallas_call(
        _norm_proj_kernel,
        out_shape=jax.ShapeDtypeStruct((m, n), BF16),
        grid=(m // tm, n // tn),
        in_specs=[
            pl.BlockSpec((tm, d), lambda i, j: (i, 0)),
            pl.BlockSpec((None, 1, d), lambda i, j: (layer, 0, 0)),
            pl.BlockSpec((None, d, tn),
                         lambda i, j: (layer, 0, jnp.where(j < front, j, j + skip))),
            pl.BlockSpec((tm, RET_QK_DIM), lambda i, j: (i % pos_blocks, 0)),
            pl.BlockSpec((tm, RET_QK_DIM), lambda i, j: (i % pos_blocks, 0)),
        ],
        out_specs=pl.BlockSpec((tm, tn), lambda i, j: (i, j)),
        scratch_shapes=[pltpu.VMEM((tm, d), BF16)],
        compiler_params=_params(("parallel", "arbitrary")),
        name="norm_proj",
    )(x2, gain, w_in, cos, sin)


def _retention_kernel(q_ref, k_ref, v_ref, decay_ref, qdec_ref, kdec_ref, o_ref,
                      state_ref, *, chunk_decay, chunks):
    n = pl.program_id(1)

    @pl.when(n == 0)
    def _():
        state_ref[...] = jnp.zeros_like(state_ref)

    states = [state_ref[h] for h in range(RET_HEADS)]
    items = [(c, h) for c in range(chunks) for h in range(RET_HEADS)]
    n = len(items)

    def slices(item):
        c, h = item
        return (slice(c * RET_CHUNK, (c + 1) * RET_CHUNK),
                slice(h * RET_QK_DIM, (h + 1) * RET_QK_DIM),
                slice(h * RET_V_DIM, (h + 1) * RET_V_DIM))

    first, second = {}, {}
    for i in range(n + 2):
        if i < n:
            rows, qk, vv = slices(items[i])
            h = items[i][1]
            q = q_ref[rows, qk]
            k = k_ref[rows, qk]
            v = v_ref[rows, vv]
            scores = _dot_nt(q, k) * decay_ref[h]
            cross = _dot(q, states[h].astype(BF16)) * qdec_ref[:, vv]
            kd = (k.astype(F32) * kdec_ref[:, qk]).astype(BF16)
            states[h] = states[h] * chunk_decay[h] + _dot_tn(kd, v)
            first[i] = (scores.astype(BF16), cross)
        if 0 <= i - 1 < n:
            rows, qk, vv = slices(items[i - 1])
            scores, cross = first.pop(i - 1)
            second[i - 1] = _dot(scores, v_ref[rows, vv]) + cross
        if 0 <= i - 2 < n:
            rows, qk, vv = slices(items[i - 2])
            y = second.pop(i - 2)
            mu = jnp.mean(y, axis=-1, keepdims=True)
            yc = y - mu
            var = jnp.mean(yc * yc, axis=-1, keepdims=True)
            o_ref[rows, vv] = yc * lax.rsqrt(var + GN_EPS)
    for h in range(RET_HEADS):
        state_ref[h] = states[h]


def _retention(pa, decay, qdec, kdec, chunk_decay, *, batch, seq, chunks=4):
    rows = chunks * RET_CHUNK
    steps = seq // rows
    m = pa.shape[0]
    row = lambda b, n: b * steps + n
    return pl.pallas_call(
        functools.partial(_retention_kernel, chunk_decay=chunk_decay, chunks=chunks),
        out_shape=jax.ShapeDtypeStruct((m, RET_V), F32),
        grid=(batch, steps),
        in_specs=[
            pl.BlockSpec((rows, RET_QK), lambda b, n: (row(b, n), 0)),
            pl.BlockSpec((rows, RET_QK), lambda b, n: (row(b, n), 1)),
            pl.BlockSpec((rows, RET_V), lambda b, n: (row(b, n), 1)),
            pl.BlockSpec((RET_HEADS, RET_CHUNK, RET_CHUNK), lambda b, n: (0, 0, 0)),
            pl.BlockSpec((RET_CHUNK, RET_V), lambda b, n: (0, 0)),
            pl.BlockSpec((RET_CHUNK, RET_QK), lambda b, n: (0, 0)),
        ],
        out_specs=pl.BlockSpec((rows, RET_V), lambda b, n: (row(b, n), 0)),
        scratch_shapes=[pltpu.VMEM((RET_HEADS, RET_QK_DIM, RET_V_DIM), F32)],
        compiler_params=_params(("parallel", "arbitrary")),
        name="retention",
    )(pa, pa, pa, decay, qdec, kdec)


LOG2_E = 1.4426950408889634
SB_EXP2_UNDERFLOW = -150.0
SKEW_A = 2
SKEW_B = 4


def _sb_logits(q, kb):
    return _dot_nt(q, kb) * (SB_HEAD_DIM ** -0.5 * LOG2_E)


def _sb_survival(u, tri, mask):
    softplus = jnp.log2(1.0 + jnp.exp2(-jnp.abs(u)))
    log_beta = jnp.minimum(u, 0.0) - softplus
    log_1m = log_beta - u
    if mask is not None:
        log_1m = jnp.where(mask, log_1m, 0.0)
    hi = log_1m.astype(BF16)
    lo = (log_1m - hi.astype(F32)).astype(BF16)
    after = _dot(hi, tri) + _dot(lo, tri)
    return log_beta + after, jnp.sum(log_1m, axis=-1, keepdims=True)


def _sb_kernel(q_ref, k_ref, v_ref, tri_ref, o_ref, *, tb, heads):
    ti = pl.program_id(2)
    tri = tri_ref[...]
    rows = lax.broadcasted_iota(jnp.int32, (tb, tb), 0)
    cols = lax.broadcasted_iota(jnp.int32, (tb, tb), 1)
    causal = cols < rows
    start = pl.multiple_of(ti * tb, tb)

    def head(g):
        return slice(g * SB_HEAD_DIM, (g + 1) * SB_HEAD_DIM)

    def full_tile(g, j, carry, acc):
        off = pl.multiple_of(j * tb, tb)
        e, s = _sb_survival(_sb_logits(q_ref[:, head(g)], k_ref[pl.ds(off, tb), head(g)]),
                            tri, None)
        w = jnp.exp2(e + carry)
        return carry + s, acc + _dot(w.astype(BF16), v_ref[pl.ds(off, tb), head(g)])

    def static_tiles(previous):
        tiles = [(g, kind) for g in range(heads) for kind in (("D", "P") if previous else ("D",))]
        n = len(tiles)
        offset = {"D": start, "P": pl.multiple_of((ti - 1) * tb, tb)}
        logits, expo, carry, acc = {}, {}, {}, {}
        for i in range(n + SKEW_B):
            if i < n:
                g, kind = tiles[i]
                logits[i] = _sb_logits(q_ref[:, head(g)],
                                       k_ref[pl.ds(offset[kind], tb), head(g)])
            if 0 <= i - SKEW_A < n:
                mask = causal if tiles[i - SKEW_A][1] == "D" else None
                expo[i - SKEW_A] = _sb_survival(logits.pop(i - SKEW_A), tri, mask)
            if 0 <= i - SKEW_B < n:
                g, kind = tiles[i - SKEW_B]
                e, s = expo.pop(i - SKEW_B)
                vb = v_ref[pl.ds(offset[kind], tb), head(g)]
                if kind == "D":
                    w = jnp.where(causal, jnp.exp2(e), 0.0)
                    acc[g] = _dot(w.astype(BF16), vb)
                    carry[g] = s
                else:
                    w = jnp.exp2(e + carry[g])
                    acc[g] = acc[g] + _dot(w.astype(BF16), vb)
                    carry[g] = carry[g] + s
        return [(carry[g], acc[g]) for g in range(heads)]

    @pl.when(ti == 0)
    def _():
        for g, (_, acc) in enumerate(static_tiles(previous=False)):
            o_ref[:, head(g)] = acc.astype(o_ref.dtype)

    @pl.when(ti > 0)
    def _():
        def alive(carry):
            return jnp.max(carry) > SB_EXP2_UNDERFLOW

        state = static_tiles(previous=True)
        flags = [alive(carry) for carry, _ in state]
        for g, (carry, acc) in enumerate(state):
            def cond(c):
                return jnp.logical_and(c[0] >= 0, c[3])

            def body(c, g=g):
                carry, acc = full_tile(g, c[0], c[1], c[2])
                return c[0] - 1, carry, acc, alive(carry)

            _, _, acc, _ = lax.while_loop(cond, body, (ti - 2, carry, acc, flags[g]))
            o_ref[:, head(g)] = acc.astype(o_ref.dtype)


def _stick_breaking(pa, tri, *, batch, seq, tb=256, heads=8):
    m = pa.shape[0]
    t_blocks = seq // tb
    width = heads * SB_HEAD_DIM
    q_col = (2 * RET_QK + RET_V) // width
    k_col = q_col + SB_W // width
    v_col = k_col + SB_W // width
    return pl.pallas_call(
        functools.partial(_sb_kernel, tb=tb, heads=heads),
        out_shape=jax.ShapeDtypeStruct((m, SB_W), BF16),
        grid=(batch, SB_HEADS // heads, t_blocks),
        in_specs=[
            pl.BlockSpec((tb, width), lambda b, h, t: (b * t_blocks + t, q_col + h)),
            pl.BlockSpec((seq, width), lambda b, h, t: (b, k_col + h)),
            pl.BlockSpec((seq, width), lambda b, h, t: (b, v_col + h)),
            pl.BlockSpec((tb, tb), lambda b, h, t: (0, 0)),
        ],
        out_specs=pl.BlockSpec((tb, width), lambda b, h, t: (b * t_blocks + t, h)),
        compiler_params=_params(("parallel", "parallel", "arbitrary")),
        name="stick_breaking",
    )(pa, pa, pa, tri)


def _merge_kernel(ret_ref, sb_ref, x_ref, g_ref, wsw_ref, wg_ref, pr_ref, ps_ref, wo_ref, o_ref):
    d = x_ref.shape[1]
    xf = x_ref[...]
    h = _rms(xf, g_ref[...]).astype(BF16)
    swish = _dot(h, wsw_ref[...])
    gates = _dot(h, wg_ref[...])
    gated_ret = ret_ref[...] * (swish * jax.nn.sigmoid(swish))
    branch_a = _dot(gated_ret.astype(BF16), pr_ref[...])
    branch_b = _dot(sb_ref[...], ps_ref[...])
    merged = (jax.nn.sigmoid(gates[:, :d]) * branch_a
              + jax.nn.sigmoid(gates[:, d:]) * branch_b)
    o_ref[...] = xf + _dot(merged.astype(BF16), wo_ref[...])


def _merge(ret, sb, x2, gain, w_in, p_ret, p_sb, w_out, layer, *, tm=512):
    m, d = x2.shape
    row = lambda i: (i, 0)
    per_layer = lambda i: (layer, 0, 0)
    swish_col = (2 * RET_QK + RET_V) // RET_V
    gate_col = (2 * RET_QK + 2 * RET_V + 3 * SB_W) // (2 * d)

    def weight(shape, index_map):
        return pl.BlockSpec(shape, index_map, pipeline_mode=pl.Buffered(1))

    return pl.pallas_call(
        _merge_kernel,
        out_shape=jax.ShapeDtypeStruct((m, d), F32),
        grid=(m // tm,),
        in_specs=[
            pl.BlockSpec((tm, RET_V), row),
            pl.BlockSpec((tm, SB_W), row),
            pl.BlockSpec((tm, d), row),
            pl.BlockSpec((None, 1, d), per_layer),
            weight((None, d, RET_V), lambda i: (layer, 0, swish_col)),
            weight((None, d, 2 * d), lambda i: (layer, 0, gate_col)),
            weight((None, RET_V, d), per_layer),
            weight((None, SB_W, d), per_layer),
            weight((None, d, d), per_layer),
        ],
        out_specs=pl.BlockSpec((tm, d), row),
        compiler_params=_params(("parallel",)),
        name="gate_merge",
    )(ret, sb, x2, gain, w_in, w_in, p_ret, p_sb, w_out)


def _mlp_kernel(x_ref, g_ref, wu_ref, wd_ref, gf_ref, o_ref, h_ref, *, final_norm):
    j = pl.program_id(1)

    @pl.when(j == 0)
    def _():
        xf = x_ref[...]
        h_ref[...] = _rms(xf, g_ref[...]).astype(BF16)
        o_ref[...] = xf

    up = jnp.maximum(_dot(h_ref[...], wu_ref[...]), 0.0)
    o_ref[...] += _dot((up * up).astype(BF16), wd_ref[...])

    if final_norm:
        @pl.when(j == pl.num_programs(1) - 1)
        def _():
            gain = gf_ref[...]

            def rows(c, carry):
                r = pl.ds(pl.multiple_of(c * NORM_ROWS, NORM_ROWS), NORM_ROWS)
                o_ref[r, :] = _rms(o_ref[r, :], gain)
                return carry

            lax.fori_loop(0, o_ref.shape[0] // NORM_ROWS, rows, 0)


def _mlp(x2, gain, w_up, w_down, g_final, layer, *, final_norm, tm=1024, tf=1024):
    m, d = x2.shape
    ff = w_up.shape[2]
    return pl.pallas_call(
        functools.partial(_mlp_kernel, final_norm=final_norm),
        out_shape=jax.ShapeDtypeStruct((m, d), F32),
        grid=(m // tm, ff // tf),
        in_specs=[
            pl.BlockSpec((tm, d), lambda i, j: (i, 0)),
            pl.BlockSpec((None, 1, d), lambda i, j: (layer, 0, 0)),
            pl.BlockSpec((None, d, tf), lambda i, j: (layer, 0, j)),
            pl.BlockSpec((None, tf, d), lambda i, j: (layer, j, 0)),
            pl.BlockSpec((1, d), lambda i, j: (0, 0)),
        ],
        out_specs=pl.BlockSpec((tm, d), lambda i, j: (i, 0)),
        scratch_shapes=[pltpu.VMEM((tm, d), BF16)],
        compiler_params=_params(("parallel", "arbitrary")),
        name="relu2_mlp_final" if final_norm else "relu2_mlp",
    )(x2, gain, w_up, w_down, g_final)


def _tables(seq, tb):
    half = RET_QK_DIM // 2
    inv_freq = ROPE_BASE ** (-jnp.arange(half, dtype=F32) / half)
    ang = jnp.arange(seq, dtype=F32)[:, None] * inv_freq[None, :]
    cos = jnp.cos(ang)
    sin = jnp.sin(ang)
    cos_t = jnp.concatenate([cos, cos], axis=-1)
    sin_t = jnp.concatenate([-sin, sin], axis=-1)

    log_gamma = jnp.log1p(-(2.0 ** (-5.0 - jnp.arange(RET_HEADS, dtype=F32))))
    c = RET_CHUNK
    idx = jnp.arange(c, dtype=F32)
    diff = idx[:, None] - idx[None, :]
    decay = jnp.where(diff[None] >= 0,
                      jnp.exp(log_gamma[:, None, None] * jnp.maximum(diff, 0.0)[None]), 0.0)
    q_decay = jnp.exp(log_gamma[None, :] * (idx[:, None] + 1.0))
    k_decay = jnp.exp(log_gamma[None, :] * (c - 1.0 - idx[:, None]))
    qdec = jnp.repeat(q_decay, RET_V_DIM, axis=1)
    kdec = jnp.repeat(k_decay, RET_QK_DIM, axis=1)
    gamma = 1.0 - 2.0 ** (-5.0 - np.arange(RET_HEADS, dtype=np.float64))
    chunk_decay = tuple(float(g) for g in gamma ** c)

    s_idx = jnp.arange(tb)
    tri = (s_idx[:, None] > s_idx[None, :]).astype(BF16)
    return cos_t, sin_t, decay, qdec, kdec, chunk_decay, tri


def kernel(x, w_in, p_ret, p_sb, w_out, w_up, w_down, g_mix, g_mlp, g_final):
    batch, seq, d = x.shape
    depth = w_in.shape[0]
    sb_tb = 256
    cos_t, sin_t, decay, qdec, kdec, chunk_decay, tri = _tables(seq, sb_tb)

    w_in_b = w_in.astype(BF16)
    p_ret_b = p_ret.astype(BF16)
    p_sb_b = p_sb.astype(BF16)
    w_out_b = w_out.astype(BF16)
    w_up_b = w_up.astype(BF16)
    w_down_b = w_down.astype(BF16)
    g_mix3 = g_mix.reshape(depth, 1, d)
    g_mlp3 = g_mlp.reshape(depth, 1, d)

    x2 = x.reshape(batch * seq, d)
    g_fin = g_final.reshape(1, d)
    for l in range(depth):
        pa = _norm_proj(x2, g_mix3, w_in_b, l, cos_t, sin_t)
        ret = _retention(pa, decay, qdec, kdec, chunk_decay, batch=batch, seq=seq)
        sb = _stick_breaking(pa, tri, batch=batch, seq=seq, tb=sb_tb)
        x2 = _merge(ret, sb, x2, g_mix3, w_in_b, p_ret_b, p_sb_b, w_out_b, l)
        x2 = _mlp(x2, g_mlp3, w_up_b, w_down_b, g_fin, l, final_norm=(l == depth - 1))
    return x2.reshape(batch, seq, d)
```

```python
import functools

import numpy as np
import jax
import jax.numpy as jnp
from jax import lax
from jax.experimental import pallas as pl
from jax.experimental.pallas import tpu as pltpu

D_MODEL = 1024
RET_HEADS = 4
RET_QK_DIM = 128
RET_V_DIM = 256
RET_CHUNK = 128
SB_HEADS = 8
SB_HEAD_DIM = 128
D_FF = 4 * D_MODEL
ROPE_BASE = 10000.0
NORM_EPS = 1e-6
GN_EPS = 1e-5

RET_QK = RET_HEADS * RET_QK_DIM
RET_V = RET_HEADS * RET_V_DIM
SB_W = SB_HEADS * SB_HEAD_DIM

VMEM_LIMIT_BYTES = 56 * 1024 * 1024

BF16 = jnp.bfloat16
F32 = jnp.float32


def _params(semantics):
    return pltpu.CompilerParams(dimension_semantics=semantics,
                                vmem_limit_bytes=VMEM_LIMIT_BYTES)


def _dot(a, b):
    return jnp.dot(a, b, preferred_element_type=F32)


def _dot_nt(a, b):
    return lax.dot_general(a, b, (((1,), (1,)), ((), ())), preferred_element_type=F32)


def _dot_tn(a, b):
    return lax.dot_general(a, b, (((0,), (0,)), ((), ())), preferred_element_type=F32)


def _rms(xf, gain):
    ms = jnp.mean(xf * xf, axis=-1, keepdims=True)
    return xf * lax.rsqrt(ms + NORM_EPS) * gain


NORM_ROWS = 256
PROJ_ROWS = 256


def _norm_proj_kernel(x_ref, g_ref, w_ref, cos_ref, sin_ref, o_ref, h_ref):
    j = pl.program_id(1)
    tm, tn = o_ref.shape
    n_chunks = tm // PROJ_ROWS
    w = w_ref[...]

    def chunk_rows(c):
        return slice(c * PROJ_ROWS, (c + 1) * PROJ_ROWS)

    def rotate_store(c, acc, cos, sin, cos_k, sin_k):
        rows = chunk_rows(c)
        for g in range(tn // RET_QK_DIM):
            sl = slice(g * RET_QK_DIM, (g + 1) * RET_QK_DIM)
            t = acc[:, sl]
            if g < RET_HEADS:
                r = t * cos + pltpu.roll(t, RET_QK_DIM // 2, axis=1) * sin
            else:
                r = t * cos_k + pltpu.roll(t, RET_QK_DIM // 2, axis=1) * sin_k
            o_ref[rows, sl] = r.astype(o_ref.dtype)

    @pl.when(j == 0)
    def _():
        gain = g_ref[...]
        h_ref[chunk_rows(0), :] = _rms(x_ref[chunk_rows(0), :], gain).astype(BF16)
        for c in range(n_chunks):
            acc = _dot(h_ref[chunk_rows(c), :], w)
            if c + 1 < n_chunks:
                nxt = chunk_rows(c + 1)
                h_ref[nxt, :] = _rms(x_ref[nxt, :], gain).astype(BF16)
            cos = cos_ref[chunk_rows(c), :]
            sin = sin_ref[chunk_rows(c), :]
            rotate_store(c, acc, cos, sin, cos * (RET_QK_DIM ** -0.5), sin * (RET_QK_DIM ** -0.5))

    @pl.when(j != 0)
    def _():
        for c in range(n_chunks):
            o_ref[chunk_rows(c), :] = _dot(h_ref[chunk_rows(c), :], w).astype(o_ref.dtype)


def _norm_proj(x2, gain, w_in, layer, cos, sin, *, tm=2048, tn=2 * RET_QK):
    m, d = x2.shape
    seq = cos.shape[0]
    pos_blocks = seq // tm
    front = (2 * RET_QK + RET_V) // tn
    skip = RET_V // tn
    n = 2 * RET_QK + RET_V + 3 * SB_W
    return pl.pallas_call(
        _norm_proj_kernel,
        out_shape=jax.ShapeDtypeStruct((m, n), BF16),
        grid=(m // tm, n // tn),
        in_specs=[
            pl.BlockSpec((tm, d), lambda i, j: (i, 0)),
            pl.BlockSpec((None, 1, d), lambda i, j: (layer, 0, 0)),
            pl.BlockSpec((None, d, tn),
                         lambda i, j: (layer, 0, jnp.where(j < front, j, j + skip))),
            pl.BlockSpec((tm, RET_QK_DIM), lambda i, j: (i % pos_blocks, 0)),
            pl.BlockSpec((tm, RET_QK_DIM), lambda i, j: (i % pos_blocks, 0)),
        ],
        out_specs=pl.BlockSpec((tm, tn), lambda i, j: (i, j)),
        scratch_shapes=[pltpu.VMEM((tm, d), BF16)],
        compiler_params=_params(("parallel", "arbitrary")),
        name="norm_proj",
    )(x2, gain, w_in, cos, sin)


def _retention_kernel(q_ref, k_ref, v_ref, decay_ref, qdec_ref, kdec_ref, o_ref,
                      state_ref, *, chunk_decay, chunks):
    n = pl.program_id(1)

    @pl.when(n == 0)
    def _():
        state_ref[...] = jnp.zeros_like(state_ref)

    states = [state_ref[h] for h in range(RET_HEADS)]
    items = [(c, h) for c in range(chunks) for h in range(RET_HEADS)]
    n = len(items)

    def slices(item):
        c, h = item
        return (slice(c * RET_CHUNK, (c + 1) * RET_CHUNK),
                slice(h * RET_QK_DIM, (h + 1) * RET_QK_DIM),
                slice(h * RET_V_DIM, (h + 1) * RET_V_DIM))

    first, second = {}, {}
    for i in range(n + 2):
        if i < n:
            rows, qk, vv = slices(items[i])
            h = items[i][1]
            q = q_ref[rows, qk]
            k = k_ref[rows, qk]
            v = v_ref[rows, vv]
            scores = _dot_nt(q, k) * decay_ref[h]
            cross = _dot(q, states[h].astype(BF16)) * qdec_ref[:, vv]
            kd = (k.astype(F32) * kdec_ref[:, qk]).astype(BF16)
            states[h] = states[h] * chunk_decay[h] + _dot_tn(kd, v)
            first[i] = (scores.astype(BF16), cross)
        if 0 <= i - 1 < n:
            rows, qk, vv = slices(items[i - 1])
            scores, cross = first.pop(i - 1)
            second[i - 1] = _dot(scores, v_ref[rows, vv]) + cross
        if 0 <= i - 2 < n:
            rows, qk, vv = slices(items[i - 2])
            y = second.pop(i - 2)
            mu = jnp.mean(y, axis=-1, keepdims=True)
            yc = y - mu
            var = jnp.mean(yc * yc, axis=-1, keepdims=True)
            o_ref[rows, vv] = yc * lax.rsqrt(var + GN_EPS)
    for h in range(RET_HEADS):
        state_ref[h] = states[h]


def _retention(pa, decay, qdec, kdec, chunk_decay, *, batch, seq, chunks=4):
    rows = chunks * RET_CHUNK
    steps = seq // rows
    m = pa.shape[0]
    row = lambda b, n: b * steps + n
    return pl.pallas_call(
        functools.partial(_retention_kernel, chunk_decay=chunk_decay, chunks=chunks),
        out_shape=jax.ShapeDtypeStruct((m, RET_V), F32),
        grid=(batch, steps),
        in_specs=[
            pl.BlockSpec((rows, RET_QK), lambda b, n: (row(b, n), 0)),
            pl.BlockSpec((rows, RET_QK), lambda b, n: (row(b, n), 1)),
            pl.BlockSpec((rows, RET_V), lambda b, n: (row(b, n), 1)),
            pl.BlockSpec((RET_HEADS, RET_CHUNK, RET_CHUNK), lambda b, n: (0, 0, 0)),
            pl.BlockSpec((RET_CHUNK, RET_V), lambda b, n: (0, 0)),
            pl.BlockSpec((RET_CHUNK, RET_QK), lambda b, n: (0, 0)),
        ],
        out_specs=pl.BlockSpec((rows, RET_V), lambda b, n: (row(b, n), 0)),
        scratch_shapes=[pltpu.VMEM((RET_HEADS, RET_QK_DIM, RET_V_DIM), F32)],
        compiler_params=_params(("parallel", "arbitrary")),
        name="retention",
    )(pa, pa, pa, decay, qdec, kdec)


LOG2_E = 1.4426950408889634
SB_EXP2_UNDERFLOW = -150.0
SKEW_A = 2
SKEW_B = 4


def _sb_logits(q, kb):
    return _dot_nt(q, kb) * (SB_HEAD_DIM ** -0.5 * LOG2_E)


def _sb_survival(u, tri, mask):
    softplus = jnp.log2(1.0 + jnp.exp2(-jnp.abs(u)))
    log_beta = jnp.minimum(u, 0.0) - softplus
    log_1m = log_beta - u
    if mask is not None:
        log_1m = jnp.where(mask, log_1m, 0.0)
    hi = log_1m.astype(BF16)
    lo = (log_1m - hi.astype(F32)).astype(BF16)
    after = _dot(hi, tri) + _dot(lo, tri)
    return log_beta + after, jnp.sum(log_1m, axis=-1, keepdims=True)


def _sb_kernel(q_ref, k_ref, v_ref, tri_ref, o_ref, *, tb, heads):
    ti = pl.program_id(2)
    tri = tri_ref[...]
    rows = lax.broadcasted_iota(jnp.int32, (tb, tb), 0)
    cols = lax.broadcasted_iota(jnp.int32, (tb, tb), 1)
    causal = cols < rows
    start = pl.multiple_of(ti * tb, tb)

    def head(g):
        return slice(g * SB_HEAD_DIM, (g + 1) * SB_HEAD_DIM)

    def full_tile(g, j, carry, acc):
        off = pl.multiple_of(j * tb, tb)
        e, s = _sb_survival(_sb_logits(q_ref[:, head(g)], k_ref[pl.ds(off, tb), head(g)]),
                            tri, None)
        w = jnp.exp2(e + carry)
        return carry + s, acc + _dot(w.astype(BF16), v_ref[pl.ds(off, tb), head(g)])

    def static_tiles(previous):
        tiles = [(g, kind) for g in range(heads) for kind in (("D", "P") if previous else ("D",))]
        n = len(tiles)
        offset = {"D": start, "P": pl.multiple_of((ti - 1) * tb, tb)}
        logits, expo, carry, acc = {}, {}, {}, {}
        for i in range(n + SKEW_B):
            if i < n:
                g, kind = tiles[i]
                logits[i] = _sb_logits(q_ref[:, head(g)],
                                       k_ref[pl.ds(offset[kind], tb), head(g)])
            if 0 <= i - SKEW_A < n:
                mask = causal if tiles[i - SKEW_A][1] == "D" else None
                expo[i - SKEW_A] = _sb_survival(logits.pop(i - SKEW_A), tri, mask)
            if 0 <= i - SKEW_B < n:
                g, kind = tiles[i - SKEW_B]
                e, s = expo.pop(i - SKEW_B)
                vb = v_ref[pl.ds(offset[kind], tb), head(g)]
                if kind == "D":
                    w = jnp.where(causal, jnp.exp2(e), 0.0)
                    acc[g] = _dot(w.astype(BF16), vb)
                    carry[g] = s
                else:
                    w = jnp.exp2(e + carry[g])
                    acc[g] = acc[g] + _dot(w.astype(BF16), vb)
                    carry[g] = carry[g] + s
        return [(carry[g], acc[g]) for g in range(heads)]

    @pl.when(ti == 0)
    def _():
        for g, (_, acc) in enumerate(static_tiles(previous=False)):
            o_ref[:, head(g)] = acc.astype(o_ref.dtype)

    @pl.when(ti > 0)
    def _():
        def alive(carry):
            return jnp.max(carry) > SB_EXP2_UNDERFLOW

        state = static_tiles(previous=True)
        flags = [alive(carry) for carry, _ in state]
        for g, (carry, acc) in enumerate(state):
            def cond(c):
                return jnp.logical_and(c[0] >= 0, c[3])

            def body(c, g=g):
                carry, acc = full_tile(g, c[0], c[1], c[2])
                return c[0] - 1, carry, acc, alive(carry)

            _, _, acc, _ = lax.while_loop(cond, body, (ti - 2, carry, acc, flags[g]))
            o_ref[:, head(g)] = acc.astype(o_ref.dtype)


def _stick_breaking(pa, tri, *, batch, seq, tb=256, heads=8):
    m = pa.shape[0]
    t_blocks = seq // tb
    width = heads * SB_HEAD_DIM
    q_col = (2 * RET_QK + RET_V) // width
    k_col = q_col + SB_W // width
    v_col = k_col + SB_W // width
    return pl.pallas_call(
        functools.partial(_sb_kernel, tb=tb, heads=heads),
        out_shape=jax.ShapeDtypeStruct((m, SB_W), BF16),
        grid=(batch, SB_HEADS // heads, t_blocks),
        in_specs=[
            pl.BlockSpec((tb, width), lambda b, h, t: (b * t_blocks + t, q_col + h)),
            pl.BlockSpec((seq, width), lambda b, h, t: (b, k_col + h)),
            pl.BlockSpec((seq, width), lambda b, h, t: (b, v_col + h)),
            pl.BlockSpec((tb, tb), lambda b, h, t: (0, 0)),
        ],
        out_specs=pl.BlockSpec((tb, width), lambda b, h, t: (b * t_blocks + t, h)),
        compiler_params=_params(("parallel", "parallel", "arbitrary")),
        name="stick_breaking",
    )(pa, pa, pa, tri)


MERGE_ROWS = 256


def _merge_kernel(ret_ref, sb_ref, x_ref, g_ref, wsw_ref, wg_ref, pr_ref, ps_ref, wo_ref, o_ref):
    tm, d = x_ref.shape
    gain = g_ref[...]
    chunks = [slice(c * MERGE_ROWS, (c + 1) * MERGE_ROWS) for c in range(tm // MERGE_ROWS)]
    hs = [_rms(x_ref[r, :], gain).astype(BF16) for r in chunks]
    swish = [_dot(h, wsw_ref[...]) for h in hs]
    gates = [_dot(h, wg_ref[...]) for h in hs]
    branch_b = [_dot(sb_ref[r, :], ps_ref[...]) for r in chunks]
    gated = [(ret_ref[r, :] * (s * jax.nn.sigmoid(s))).astype(BF16) for r, s in zip(chunks, swish)]
    branch_a = [_dot(g, pr_ref[...]) for g in gated]
    merged = [(jax.nn.sigmoid(gt[:, :d]) * a + jax.nn.sigmoid(gt[:, d:]) * b).astype(BF16)
              for gt, a, b in zip(gates, branch_a, branch_b)]
    for r, mg in zip(chunks, merged):
        o_ref[r, :] = x_ref[r, :] + _dot(mg, wo_ref[...])


def _merge(ret, sb, x2, gain, w_in, p_ret, p_sb, w_out, layer, *, tm=512):
    m, d = x2.shape
    row = lambda i: (i, 0)
    per_layer = lambda i: (layer, 0, 0)
    swish_col = (2 * RET_QK + RET_V) // RET_V
    gate_col = (2 * RET_QK + 2 * RET_V + 3 * SB_W) // (2 * d)

    def weight(shape, index_map):
        return pl.BlockSpec(shape, index_map, pipeline_mode=pl.Buffered(1))

    return pl.pallas_call(
        _merge_kernel,
        out_shape=jax.ShapeDtypeStruct((m, d), F32),
        grid=(m // tm,),
        in_specs=[
            pl.BlockSpec((tm, RET_V), row),
            pl.BlockSpec((tm, SB_W), row),
            pl.BlockSpec((tm, d), row),
            pl.BlockSpec((None, 1, d), per_layer),
            weight((None, d, RET_V), lambda i: (layer, 0, swish_col)),
            weight((None, d, 2 * d), lambda i: (layer, 0, gate_col)),
            weight((None, RET_V, d), per_layer),
            weight((None, SB_W, d), per_layer),
            weight((None, d, d), per_layer),
        ],
        out_specs=pl.BlockSpec((tm, d), row),
        compiler_params=_params(("parallel",)),
        name="gate_merge",
    )(ret, sb, x2, gain, w_in, w_in, p_ret, p_sb, w_out)


def _mlp_kernel(x_ref, g_ref, wu_ref, wd_ref, gf_ref, o_ref, h_ref, *, final_norm):
    j = pl.program_id(1)

    def mlp(h):
        up = jnp.maximum(_dot(h, wu_ref[...]), 0.0)
        return _dot((up * up).astype(BF16), wd_ref[...])

    @pl.when(j == 0)
    def _():
        gain = g_ref[...]
        n_chunks = x_ref.shape[0] // NORM_ROWS

        def rows(c):
            return slice(c * NORM_ROWS, (c + 1) * NORM_ROWS)

        h_ref[rows(0), :] = _rms(x_ref[rows(0), :], gain).astype(BF16)
        for c in range(n_chunks):
            y = mlp(h_ref[rows(c), :])
            if c + 1 < n_chunks:
                h_ref[rows(c + 1), :] = _rms(x_ref[rows(c + 1), :], gain).astype(BF16)
            o_ref[rows(c), :] = x_ref[rows(c), :] + y

    @pl.when(j != 0)
    def _():
        o_ref[...] += mlp(h_ref[...])

    if final_norm:
        @pl.when(j == pl.num_programs(1) - 1)
        def _():
            gain = gf_ref[...]

            def rows(c, carry):
                r = pl.ds(pl.multiple_of(c * NORM_ROWS, NORM_ROWS), NORM_ROWS)
                o_ref[r, :] = _rms(o_ref[r, :], gain)
                return carry

            lax.fori_loop(0, o_ref.shape[0] // NORM_ROWS, rows, 0)


def _mlp(x2, gain, w_up, w_down, g_final, layer, *, final_norm, tm=1024, tf=1024):
    m, d = x2.shape
    ff = w_up.shape[2]
    return pl.pallas_call(
        functools.partial(_mlp_kernel, final_norm=final_norm),
        out_shape=jax.ShapeDtypeStruct((m, d), F32),
        grid=(m // tm, ff // tf),
        in_specs=[
            pl.BlockSpec((tm, d), lambda i, j: (i, 0)),
            pl.BlockSpec((None, 1, d), lambda i, j: (layer, 0, 0)),
            pl.BlockSpec((None, d, tf), lambda i, j: (layer, 0, j)),
            pl.BlockSpec((None, tf, d), lambda i, j: (layer, j, 0)),
            pl.BlockSpec((1, d), lambda i, j: (0, 0)),
        ],
        out_specs=pl.BlockSpec((tm, d), lambda i, j: (i, 0)),
        scratch_shapes=[pltpu.VMEM((tm, d), BF16)],
        compiler_params=_params(("parallel", "arbitrary")),
        name="relu2_mlp_final" if final_norm else "relu2_mlp",
    )(x2, gain, w_up, w_down, g_final)


def _tables(seq, tb):
    half = RET_QK_DIM // 2
    inv_freq = ROPE_BASE ** (-jnp.arange(half, dtype=F32) / half)
    ang = jnp.arange(seq, dtype=F32)[:, None] * inv_freq[None, :]
    cos = jnp.cos(ang)
    sin = jnp.sin(ang)
    cos_t = jnp.concatenate([cos, cos], axis=-1)
    sin_t = jnp.concatenate([-sin, sin], axis=-1)

    log_gamma = jnp.log1p(-(2.0 ** (-5.0 - jnp.arange(RET_HEADS, dtype=F32))))
    c = RET_CHUNK
    idx = jnp.arange(c, dtype=F32)
    diff = idx[:, None] - idx[None, :]
    decay = jnp.where(diff[None] >= 0,
                      jnp.exp(log_gamma[:, None, None] * jnp.maximum(diff, 0.0)[None]), 0.0)
    q_decay = jnp.exp(log_gamma[None, :] * (idx[:, None] + 1.0))
    k_decay = jnp.exp(log_gamma[None, :] * (c - 1.0 - idx[:, None]))
    qdec = jnp.repeat(q_decay, RET_V_DIM, axis=1)
    kdec = jnp.repeat(k_decay, RET_QK_DIM, axis=1)
    gamma = 1.0 - 2.0 ** (-5.0 - np.arange(RET_HEADS, dtype=np.float64))
    chunk_decay = tuple(float(g) for g in gamma ** c)

    s_idx = jnp.arange(tb)
    tri = (s_idx[:, None] > s_idx[None, :]).astype(BF16)
    return cos_t, sin_t, decay, qdec, kdec, chunk_decay, tri


def kernel(x, w_in, p_ret, p_sb, w_out, w_up, w_down, g_mix, g_mlp, g_final):
    batch, seq, d = x.shape
    depth = w_in.shape[0]
    sb_tb = 256
    cos_t, sin_t, decay, qdec, kdec, chunk_decay, tri = _tables(seq, sb_tb)

    w_in_b = w_in.astype(BF16)
    p_ret_b = p_ret.astype(BF16)
    p_sb_b = p_sb.astype(BF16)
    w_out_b = w_out.astype(BF16)
    w_up_b = w_up.astype(BF16)
    w_down_b = w_down.astype(BF16)
    g_mix3 = g_mix.reshape(depth, 1, d)
    g_mlp3 = g_mlp.reshape(depth, 1, d)

    x2 = x.reshape(batch * seq, d)
    g_fin = g_final.reshape(1, d)
    for l in range(depth):
        pa = _norm_proj(x2, g_mix3, w_in_b, l, cos_t, sin_t)
        ret = _retention(pa, decay, qdec, kdec, chunk_decay, batch=batch, seq=seq)
        sb = _stick_breaking(pa, tri, batch=batch, seq=seq, tb=sb_tb)
        x2 = _merge(ret, sb, x2, g_mix3, w_in_b, p_ret_b, p_sb_b, w_out_b, l)
        x2 = _mlp(x2, g_mlp3, w_up_b, w_down_b, g_fin, l, final_norm=(l == depth - 1))
    return x2.reshape(batch, seq, d)
```

```python
import functools

import numpy as np
import jax
import jax.numpy as jnp
from jax import lax
from jax.experimental import pallas as pl
from jax.experimental.pallas import tpu as pltpu

RET_HEADS = 4
RET_QK_DIM = 128
RET_V_DIM = 256
RET_CHUNK = 128
SB_HEADS = 8
SB_HEAD_DIM = 128
ROPE_BASE = 10000.0
NORM_EPS = 1e-6
GN_EPS = 1e-5

RET_QK = RET_HEADS * RET_QK_DIM
RET_V = RET_HEADS * RET_V_DIM
SB_W = SB_HEADS * SB_HEAD_DIM

VMEM_LIMIT_BYTES = 56 * 1024 * 1024

BF16 = jnp.bfloat16
F32 = jnp.float32


def _params(semantics):
    return pltpu.CompilerParams(dimension_semantics=semantics,
                                vmem_limit_bytes=VMEM_LIMIT_BYTES)


def _dot(a, b):
    return jnp.dot(a, b, preferred_element_type=F32)


def _dot_nt(a, b):
    return lax.dot_general(a, b, (((1,), (1,)), ((), ())), preferred_element_type=F32)


def _dot_tn(a, b):
    return lax.dot_general(a, b, (((0,), (0,)), ((), ())), preferred_element_type=F32)


def _rms(xf, gain):
    ms = jnp.mean(xf * xf, axis=-1, keepdims=True)
    return xf * lax.rsqrt(ms + NORM_EPS) * gain


NORM_ROWS = 256
PROJ_ROWS = 256


def _norm_proj_kernel(x_ref, g_ref, w_ref, cos_ref, sin_ref, o_ref, h_ref):
    j = pl.program_id(1)
    tm, tn = o_ref.shape
    n_chunks = tm // PROJ_ROWS
    w = w_ref[...]

    def chunk_rows(c):
        return slice(c * PROJ_ROWS, (c + 1) * PROJ_ROWS)

    def rotate_store(c, acc, cos, sin, cos_k, sin_k):
        rows = chunk_rows(c)
        for g in range(tn // RET_QK_DIM):
            sl = slice(g * RET_QK_DIM, (g + 1) * RET_QK_DIM)
            t = acc[:, sl]
            if g < RET_HEADS:
                r = t * cos + pltpu.roll(t, RET_QK_DIM // 2, axis=1) * sin
            else:
                r = t * cos_k + pltpu.roll(t, RET_QK_DIM // 2, axis=1) * sin_k
            o_ref[rows, sl] = r.astype(o_ref.dtype)

    @pl.when(j == 0)
    def _():
        gain = g_ref[...]
        h_ref[chunk_rows(0), :] = _rms(x_ref[chunk_rows(0), :], gain).astype(BF16)
        for c in range(n_chunks):
            acc = _dot(h_ref[chunk_rows(c), :], w)
            if c + 1 < n_chunks:
                nxt = chunk_rows(c + 1)
                h_ref[nxt, :] = _rms(x_ref[nxt, :], gain).astype(BF16)
            cos = cos_ref[chunk_rows(c), :]
            sin = sin_ref[chunk_rows(c), :]
            rotate_store(c, acc, cos, sin, cos * (RET_QK_DIM ** -0.5), sin * (RET_QK_DIM ** -0.5))

    @pl.when(j != 0)
    def _():
        for c in range(n_chunks):
            o_ref[chunk_rows(c), :] = _dot(h_ref[chunk_rows(c), :], w).astype(o_ref.dtype)


def _norm_proj(x2, gain, w_in, layer, cos, sin, *, tm=2048, tn=2 * RET_QK):
    m, d = x2.shape
    seq = cos.shape[0]
    pos_blocks = seq // tm
    front = (2 * RET_QK + RET_V) // tn
    skip = RET_V // tn
    n = 2 * RET_QK + RET_V + 3 * SB_W
    return pl.pallas_call(
        _norm_proj_kernel,
        out_shape=jax.ShapeDtypeStruct((m, n), BF16),
        grid=(m // tm, n // tn),
        in_specs=[
            pl.BlockSpec((tm, d), lambda i, j: (i, 0)),
            pl.BlockSpec((None, 1, d), lambda i, j: (layer, 0, 0)),
            pl.BlockSpec((None, d, tn),
                         lambda i, j: (layer, 0, jnp.where(j < front, j, j + skip))),
            pl.BlockSpec((tm, RET_QK_DIM), lambda i, j: (i % pos_blocks, 0)),
            pl.BlockSpec((tm, RET_QK_DIM), lambda i, j: (i % pos_blocks, 0)),
        ],
        out_specs=pl.BlockSpec((tm, tn), lambda i, j: (i, j)),
        scratch_shapes=[pltpu.VMEM((tm, d), BF16)],
        compiler_params=_params(("parallel", "arbitrary")),
        name="norm_proj",
    )(x2, gain, w_in, cos, sin)


RET_SKEW = 1


def _retention_kernel(q_ref, k_ref, v_ref, decay_ref, qdec_ref, kdec_ref, o_ref,
                      state_ref, *, chunk_decay, chunks):
    n = pl.program_id(1)

    @pl.when(n == 0)
    def _():
        state_ref[...] = jnp.zeros_like(state_ref)

    states = [state_ref[h] for h in range(RET_HEADS)]
    items = [(c, h) for c in range(chunks) for h in range(RET_HEADS)]
    n = len(items)

    def slices(item):
        c, h = item
        return (slice(c * RET_CHUNK, (c + 1) * RET_CHUNK),
                slice(h * RET_QK_DIM, (h + 1) * RET_QK_DIM),
                slice(h * RET_V_DIM, (h + 1) * RET_V_DIM))

    first, second = {}, {}
    for i in range(n + 2 * RET_SKEW):
        if i < n:
            rows, qk, vv = slices(items[i])
            h = items[i][1]
            q = q_ref[rows, qk]
            k = k_ref[rows, qk]
            v = v_ref[rows, vv]
            scores = _dot_nt(q, k) * decay_ref[h]
            cross = _dot(q, states[h].astype(BF16)) * qdec_ref[:, vv]
            kd = (k.astype(F32) * kdec_ref[:, qk]).astype(BF16)
            states[h] = states[h] * chunk_decay[h] + _dot_tn(kd, v)
            first[i] = (scores.astype(BF16), cross)
        if 0 <= i - RET_SKEW < n:
            rows, qk, vv = slices(items[i - RET_SKEW])
            scores, cross = first.pop(i - RET_SKEW)
            second[i - RET_SKEW] = _dot(scores, v_ref[rows, vv]) + cross
        if 0 <= i - 2 * RET_SKEW < n:
            rows, qk, vv = slices(items[i - 2 * RET_SKEW])
            y = second.pop(i - 2 * RET_SKEW)
            mu = jnp.mean(y, axis=-1, keepdims=True)
            yc = y - mu
            var = jnp.mean(yc * yc, axis=-1, keepdims=True)
            o_ref[rows, vv] = yc * lax.rsqrt(var + GN_EPS)
    for h in range(RET_HEADS):
        state_ref[h] = states[h]


def _retention(pa, decay, qdec, kdec, chunk_decay, *, batch, seq, chunks=8):
    rows = chunks * RET_CHUNK
    steps = seq // rows
    m = pa.shape[0]
    row = lambda b, n: b * steps + n
    return pl.pallas_call(
        functools.partial(_retention_kernel, chunk_decay=chunk_decay, chunks=chunks),
        out_shape=jax.ShapeDtypeStruct((m, RET_V), F32),
        grid=(batch, steps),
        in_specs=[
            pl.BlockSpec((rows, RET_QK), lambda b, n: (row(b, n), 0)),
            pl.BlockSpec((rows, RET_QK), lambda b, n: (row(b, n), 1)),
            pl.BlockSpec((rows, RET_V), lambda b, n: (row(b, n), 1)),
            pl.BlockSpec((RET_HEADS, RET_CHUNK, RET_CHUNK), lambda b, n: (0, 0, 0)),
            pl.BlockSpec((RET_CHUNK, RET_V), lambda b, n: (0, 0)),
            pl.BlockSpec((RET_CHUNK, RET_QK), lambda b, n: (0, 0)),
        ],
        out_specs=pl.BlockSpec((rows, RET_V), lambda b, n: (row(b, n), 0)),
        scratch_shapes=[pltpu.VMEM((RET_HEADS, RET_QK_DIM, RET_V_DIM), F32)],
        compiler_params=_params(("parallel", "arbitrary")),
        name="retention",
    )(pa, pa, pa, decay, qdec, kdec)


LOG2_E = 1.4426950408889634
SB_EXP2_UNDERFLOW = -150.0
SKEW_A = 2
SKEW_B = 4


def _sb_logits(q, kb):
    return _dot_nt(q, kb) * (SB_HEAD_DIM ** -0.5 * LOG2_E)


def _sb_survival(u, tri, mask):
    softplus = jnp.log2(1.0 + jnp.exp2(-jnp.abs(u)))
    log_beta = jnp.minimum(u, 0.0) - softplus
    log_1m = log_beta - u
    if mask is not None:
        log_1m = jnp.where(mask, log_1m, 0.0)
    hi = log_1m.astype(BF16)
    lo = (log_1m - hi.astype(F32)).astype(BF16)
    after = _dot(hi, tri) + _dot(lo, tri)
    return log_beta + after, jnp.sum(log_1m, axis=-1, keepdims=True)


def _sb_kernel(q_ref, k_ref, v_ref, tri_ref, o_ref, *, tb, heads):
    ti = pl.program_id(2)
    tri = tri_ref[...]
    rows = lax.broadcasted_iota(jnp.int32, (tb, tb), 0)
    cols = lax.broadcasted_iota(jnp.int32, (tb, tb), 1)
    causal = cols < rows
    start = pl.multiple_of(ti * tb, tb)

    def head(g):
        return slice(g * SB_HEAD_DIM, (g + 1) * SB_HEAD_DIM)

    def full_tile(g, j, carry, acc):
        off = pl.multiple_of(j * tb, tb)
        e, s = _sb_survival(_sb_logits(q_ref[:, head(g)], k_ref[pl.ds(off, tb), head(g)]),
                            tri, None)
        w = jnp.exp2(e + carry)
        return carry + s, acc + _dot(w.astype(BF16), v_ref[pl.ds(off, tb), head(g)])

    def static_tiles(previous):
        tiles = [(g, kind) for g in range(heads) for kind in (("D", "P") if previous else ("D",))]
        n = len(tiles)
        offset = {"D": start, "P": pl.multiple_of((ti - 1) * tb, tb)}
        logits, expo, carry, acc = {}, {}, {}, {}
        for i in range(n + SKEW_B):
            if i < n:
                g, kind = tiles[i]
                logits[i] = _sb_logits(q_ref[:, head(g)],
                                       k_ref[pl.ds(offset[kind], tb), head(g)])
            if 0 <= i - SKEW_A < n:
                mask = causal if tiles[i - SKEW_A][1] == "D" else None
                expo[i - SKEW_A] = _sb_survival(logits.pop(i - SKEW_A), tri, mask)
            if 0 <= i - SKEW_B < n:
                g, kind = tiles[i - SKEW_B]
                e, s = expo.pop(i - SKEW_B)
                vb = v_ref[pl.ds(offset[kind], tb), head(g)]
                if kind == "D":
                    w = jnp.where(causal, jnp.exp2(e), 0.0)
                    acc[g] = _dot(w.astype(BF16), vb)
                    carry[g] = s
                else:
                    w = jnp.exp2(e + carry[g])
                    acc[g] = acc[g] + _dot(w.astype(BF16), vb)
                    carry[g] = carry[g] + s
        return [(carry[g], acc[g]) for g in range(heads)]

    @pl.when(ti == 0)
    def _():
        for g, (_, acc) in enumerate(static_tiles(previous=False)):
            o_ref[:, head(g)] = acc.astype(o_ref.dtype)

    @pl.when(ti > 0)
    def _():
        def alive(carry):
            return jnp.max(carry) > SB_EXP2_UNDERFLOW

        state = static_tiles(previous=True)
        flags = [alive(carry) for carry, _ in state]
        for g, (carry, acc) in enumerate(state):
            def cond(c):
                return jnp.logical_and(c[0] >= 0, c[3])

            def body(c, g=g):
                carry, acc = full_tile(g, c[0], c[1], c[2])
                return c[0] - 1, carry, acc, alive(carry)

            _, _, acc, _ = lax.while_loop(cond, body, (ti - 2, carry, acc, flags[g]))
            o_ref[:, head(g)] = acc.astype(o_ref.dtype)


def _stick_breaking(pa, tri, *, batch, seq, tb=256, heads=8):
    m = pa.shape[0]
    t_blocks = seq // tb
    width = heads * SB_HEAD_DIM
    q_col = (2 * RET_QK + RET_V) // width
    k_col = q_col + SB_W // width
    v_col = k_col + SB_W // width
    return pl.pallas_call(
        functools.partial(_sb_kernel, tb=tb, heads=heads),
        out_shape=jax.ShapeDtypeStruct((m, SB_W), BF16),
        grid=(batch, SB_HEADS // heads, t_blocks),
        in_specs=[
            pl.BlockSpec((tb, width), lambda b, h, t: (b * t_blocks + t, q_col + h)),
            pl.BlockSpec((seq, width), lambda b, h, t: (b, k_col + h)),
            pl.BlockSpec((seq, width), lambda b, h, t: (b, v_col + h)),
            pl.BlockSpec((tb, tb), lambda b, h, t: (0, 0)),
        ],
        out_specs=pl.BlockSpec((tb, width), lambda b, h, t: (b * t_blocks + t, h)),
        compiler_params=_params(("parallel", "parallel", "arbitrary")),
        name="stick_breaking",
    )(pa, pa, pa, tri)


MERGE_ROWS = 256


def _merge_kernel(ret_ref, sb_ref, x_ref, g_ref, wsw_ref, wg_ref, pr_ref, ps_ref, wo_ref, o_ref):
    tm, d = x_ref.shape
    gain = g_ref[...]
    chunks = [slice(c * MERGE_ROWS, (c + 1) * MERGE_ROWS) for c in range(tm // MERGE_ROWS)]
    hs = [_rms(x_ref[r, :], gain).astype(BF16) for r in chunks]
    swish = [_dot(h, wsw_ref[...]) for h in hs]
    gates = [_dot(h, wg_ref[...]) for h in hs]
    branch_b = [_dot(sb_ref[r, :], ps_ref[...]) for r in chunks]
    gated = [(ret_ref[r, :] * (s * jax.nn.sigmoid(s))).astype(BF16) for r, s in zip(chunks, swish)]
    branch_a = [_dot(g, pr_ref[...]) for g in gated]
    merged = [(jax.nn.sigmoid(gt[:, :d]) * a + jax.nn.sigmoid(gt[:, d:]) * b).astype(BF16)
              for gt, a, b in zip(gates, branch_a, branch_b)]
    for r, mg in zip(chunks, merged):
        o_ref[r, :] = x_ref[r, :] + _dot(mg, wo_ref[...])


def _merge(ret, sb, x2, gain, w_in, p_ret, p_sb, w_out, layer, *, tm=512):
    m, d = x2.shape
    row = lambda i: (i, 0)
    per_layer = lambda i: (layer, 0, 0)
    swish_col = (2 * RET_QK + RET_V) // RET_V
    gate_col = (2 * RET_QK + 2 * RET_V + 3 * SB_W) // (2 * d)

    def weight(shape, index_map):
        return pl.BlockSpec(shape, index_map, pipeline_mode=pl.Buffered(1))

    return pl.pallas_call(
        _merge_kernel,
        out_shape=jax.ShapeDtypeStruct((m, d), F32),
        grid=(m // tm,),
        in_specs=[
            pl.BlockSpec((tm, RET_V), row),
            pl.BlockSpec((tm, SB_W), row),
            pl.BlockSpec((tm, d), row),
            pl.BlockSpec((None, 1, d), per_layer),
            weight((None, d, RET_V), lambda i: (layer, 0, swish_col)),
            weight((None, d, 2 * d), lambda i: (layer, 0, gate_col)),
            weight((None, RET_V, d), per_layer),
            weight((None, SB_W, d), per_layer),
            weight((None, d, d), per_layer),
        ],
        out_specs=pl.BlockSpec((tm, d), row),
        compiler_params=_params(("parallel",)),
        name="gate_merge",
    )(ret, sb, x2, gain, w_in, w_in, p_ret, p_sb, w_out)


def _mlp_kernel(x_ref, g_ref, wu_ref, wd_ref, gf_ref, o_ref, h_ref, *, final_norm):
    j = pl.program_id(1)

    def mlp(h):
        up = jnp.maximum(_dot(h, wu_ref[...]), 0.0)
        return _dot((up * up).astype(BF16), wd_ref[...])

    @pl.when(j == 0)
    def _():
        gain = g_ref[...]
        n_chunks = x_ref.shape[0] // NORM_ROWS

        def rows(c):
            return slice(c * NORM_ROWS, (c + 1) * NORM_ROWS)

        h_ref[rows(0), :] = _rms(x_ref[rows(0), :], gain).astype(BF16)
        for c in range(n_chunks):
            y = mlp(h_ref[rows(c), :])
            if c + 1 < n_chunks:
                h_ref[rows(c + 1), :] = _rms(x_ref[rows(c + 1), :], gain).astype(BF16)
            o_ref[rows(c), :] = x_ref[rows(c), :] + y

    @pl.when(j != 0)
    def _():
        o_ref[...] += mlp(h_ref[...])

    if final_norm:
        @pl.when(j == pl.num_programs(1) - 1)
        def _():
            gain = gf_ref[...]

            def rows(c, carry):
                r = pl.ds(pl.multiple_of(c * NORM_ROWS, NORM_ROWS), NORM_ROWS)
                o_ref[r, :] = _rms(o_ref[r, :], gain)
                return carry

            lax.fori_loop(0, o_ref.shape[0] // NORM_ROWS, rows, 0)


def _mlp(x2, gain, w_up, w_down, g_final, layer, *, final_norm, tm=1024, tf=1024):
    m, d = x2.shape
    ff = w_up.shape[2]
    return pl.pallas_call(
        functools.partial(_mlp_kernel, final_norm=final_norm),
        out_shape=jax.ShapeDtypeStruct((m, d), F32),
        grid=(m // tm, ff // tf),
        in_specs=[
            pl.BlockSpec((tm, d), lambda i, j: (i, 0)),
            pl.BlockSpec((None, 1, d), lambda i, j: (layer, 0, 0)),
            pl.BlockSpec((None, d, tf), lambda i, j: (layer, 0, j)),
            pl.BlockSpec((None, tf, d), lambda i, j: (layer, j, 0)),
            pl.BlockSpec((1, d), lambda i, j: (0, 0)),
        ],
        out_specs=pl.BlockSpec((tm, d), lambda i, j: (i, 0)),
        scratch_shapes=[pltpu.VMEM((tm, d), BF16)],
        compiler_params=_params(("parallel", "arbitrary")),
        name="relu2_mlp_final" if final_norm else "relu2_mlp",
    )(x2, gain, w_up, w_down, g_final)


def _tables(seq, tb):
    half = RET_QK_DIM // 2
    inv_freq = ROPE_BASE ** (-jnp.arange(half, dtype=F32) / half)
    ang = jnp.arange(seq, dtype=F32)[:, None] * inv_freq[None, :]
    cos = jnp.cos(ang)
    sin = jnp.sin(ang)
    cos_t = jnp.concatenate([cos, cos], axis=-1)
    sin_t = jnp.concatenate([-sin, sin], axis=-1)

    log_gamma = jnp.log1p(-(2.0 ** (-5.0 - jnp.arange(RET_HEADS, dtype=F32))))
    c = RET_CHUNK
    idx = jnp.arange(c, dtype=F32)
    diff = idx[:, None] - idx[None, :]
    decay = jnp.where(diff[None] >= 0,
                      jnp.exp(log_gamma[:, None, None] * jnp.maximum(diff, 0.0)[None]), 0.0)
    q_decay = jnp.exp(log_gamma[None, :] * (idx[:, None] + 1.0))
    k_decay = jnp.exp(log_gamma[None, :] * (c - 1.0 - idx[:, None]))
    qdec = jnp.repeat(q_decay, RET_V_DIM, axis=1)
    kdec = jnp.repeat(k_decay, RET_QK_DIM, axis=1)
    gamma = 1.0 - 2.0 ** (-5.0 - np.arange(RET_HEADS, dtype=np.float64))
    chunk_decay = tuple(float(g) for g in gamma ** c)

    s_idx = jnp.arange(tb)
    tri = (s_idx[:, None] > s_idx[None, :]).astype(BF16)
    return cos_t, sin_t, decay, qdec, kdec, chunk_decay, tri


def kernel(x, w_in, p_ret, p_sb, w_out, w_up, w_down, g_mix, g_mlp, g_final):
    batch, seq, d = x.shape
    depth = w_in.shape[0]
    sb_tb = 256
    cos_t, sin_t, decay, qdec, kdec, chunk_decay, tri = _tables(seq, sb_tb)

    w_in_b = w_in.astype(BF16)
    p_ret_b = p_ret.astype(BF16)
    p_sb_b = p_sb.astype(BF16)
    w_out_b = w_out.astype(BF16)
    w_up_b = w_up.astype(BF16)
    w_down_b = w_down.astype(BF16)
    g_mix3 = g_mix.reshape(depth, 1, d)
    g_mlp3 = g_mlp.reshape(depth, 1, d)

    x2 = x.reshape(batch * seq, d)
    g_fin = g_final.reshape(1, d)
    for l in range(depth):
        pa = _norm_proj(x2, g_mix3, w_in_b, l, cos_t, sin_t)
        ret = _retention(pa, decay, qdec, kdec, chunk_decay, batch=batch, seq=seq)
        sb = _stick_breaking(pa, tri, batch=batch, seq=seq, tb=sb_tb)
        x2 = _merge(ret, sb, x2, g_mix3, w_in_b, p_ret_b, p_sb_b, w_out_b, l)
        x2 = _mlp(x2, g_mlp3, w_up_b, w_down_b, g_fin, l, final_norm=(l == depth - 1))
    return x2.reshape(batch, seq, d)
```

```python
import functools

import numpy as np
import jax
import jax.numpy as jnp
from jax import lax
from jax.experimental import pallas as pl
from jax.experimental.pallas import tpu as pltpu

RET_HEADS = 4
RET_QK_DIM = 128
RET_V_DIM = 256
RET_CHUNK = 128
SB_HEADS = 8
SB_HEAD_DIM = 128
ROPE_BASE = 10000.0
NORM_EPS = 1e-6
GN_EPS = 1e-5

RET_QK = RET_HEADS * RET_QK_DIM
RET_V = RET_HEADS * RET_V_DIM
SB_W = SB_HEADS * SB_HEAD_DIM

VMEM_LIMIT_BYTES = 56 * 1024 * 1024

BF16 = jnp.bfloat16
F32 = jnp.float32


def _params(semantics):
    return pltpu.CompilerParams(dimension_semantics=semantics,
                                vmem_limit_bytes=VMEM_LIMIT_BYTES)


def _dot(a, b):
    return jnp.dot(a, b, preferred_element_type=F32)


def _dot_nt(a, b):
    return lax.dot_general(a, b, (((1,), (1,)), ((), ())), preferred_element_type=F32)


def _dot_tn(a, b):
    return lax.dot_general(a, b, (((0,), (0,)), ((), ())), preferred_element_type=F32)


def _rms(xf, gain):
    ms = jnp.mean(xf * xf, axis=-1, keepdims=True)
    return xf * lax.rsqrt(ms + NORM_EPS) * gain


NORM_ROWS = 256
PROJ_ROWS = 256


def _norm_proj_kernel(x_ref, g_ref, w_ref, cos_ref, sin_ref, o_ref, h_ref):
    j = pl.program_id(1)
    tm, tn = o_ref.shape
    n_chunks = tm // PROJ_ROWS
    w = w_ref[...]

    def chunk_rows(c):
        return slice(c * PROJ_ROWS, (c + 1) * PROJ_ROWS)

    def rotate_store(c, acc, cos, sin, cos_k, sin_k):
        rows = chunk_rows(c)
        for g in range(tn // RET_QK_DIM):
            sl = slice(g * RET_QK_DIM, (g + 1) * RET_QK_DIM)
            t = acc[:, sl]
            if g < RET_HEADS:
                r = t * cos + pltpu.roll(t, RET_QK_DIM // 2, axis=1) * sin
            else:
                r = t * cos_k + pltpu.roll(t, RET_QK_DIM // 2, axis=1) * sin_k
            o_ref[rows, sl] = r.astype(o_ref.dtype)

    @pl.when(j == 0)
    def _():
        gain = g_ref[...]
        h_ref[chunk_rows(0), :] = _rms(x_ref[chunk_rows(0), :], gain).astype(BF16)
        for c in range(n_chunks):
            acc = _dot(h_ref[chunk_rows(c), :], w)
            if c + 1 < n_chunks:
                nxt = chunk_rows(c + 1)
                h_ref[nxt, :] = _rms(x_ref[nxt, :], gain).astype(BF16)
            cos = cos_ref[chunk_rows(c), :]
            sin = sin_ref[chunk_rows(c), :]
            rotate_store(c, acc, cos, sin, cos * (RET_QK_DIM ** -0.5), sin * (RET_QK_DIM ** -0.5))

    @pl.when(j != 0)
    def _():
        for c in range(n_chunks):
            o_ref[chunk_rows(c), :] = _dot(h_ref[chunk_rows(c), :], w).astype(o_ref.dtype)


def _norm_proj(x2, gain, w_in, layer, cos, sin, *, tm=2048, tn=2 * RET_QK):
    m, d = x2.shape
    seq = cos.shape[0]
    pos_blocks = seq // tm
    front = (2 * RET_QK + RET_V) // tn
    skip = RET_V // tn
    n = 2 * RET_QK + RET_V + 3 * SB_W
    return pl.pallas_call(
        _norm_proj_kernel,
        out_shape=jax.ShapeDtypeStruct((m, n), BF16),
        grid=(m // tm, n // tn),
        in_specs=[
            pl.BlockSpec((tm, d), lambda i, j: (i, 0)),
            pl.BlockSpec((None, 1, d), lambda i, j: (layer, 0, 0)),
            pl.BlockSpec((None, d, tn),
                         lambda i, j: (layer, 0, jnp.where(j < front, j, j + skip))),
            pl.BlockSpec((tm, RET_QK_DIM), lambda i, j: (i % pos_blocks, 0)),
            pl.BlockSpec((tm, RET_QK_DIM), lambda i, j: (i % pos_blocks, 0)),
        ],
        out_specs=pl.BlockSpec((tm, tn), lambda i, j: (i, j)),
        scratch_shapes=[pltpu.VMEM((tm, d), BF16)],
        compiler_params=_params(("parallel", "arbitrary")),
        name="norm_proj",
    )(x2, gain, w_in, cos, sin)


RET_SKEW = 1


def _retention_kernel(q_ref, k_ref, v_ref, decay_ref, qdec_ref, kdec_ref, o_ref,
                      state_ref, *, chunk_decay, chunks):
    n = pl.program_id(1)

    @pl.when(n == 0)
    def _():
        state_ref[...] = jnp.zeros_like(state_ref)

    states = [state_ref[h] for h in range(RET_HEADS)]
    items = [(c, h) for c in range(chunks) for h in range(RET_HEADS)]
    n = len(items)

    def slices(item):
        c, h = item
        return (slice(c * RET_CHUNK, (c + 1) * RET_CHUNK),
                slice(h * RET_QK_DIM, (h + 1) * RET_QK_DIM),
                slice(h * RET_V_DIM, (h + 1) * RET_V_DIM))

    first, second = {}, {}
    for i in range(n + 2 * RET_SKEW):
        if i < n:
            rows, qk, vv = slices(items[i])
            h = items[i][1]
            q = q_ref[rows, qk]
            k = k_ref[rows, qk]
            v = v_ref[rows, vv]
            scores = _dot_nt(q, k) * decay_ref[h]
            cross = _dot(q, states[h].astype(BF16)) * qdec_ref[:, vv]
            kd = (k.astype(F32) * kdec_ref[:, qk]).astype(BF16)
            states[h] = states[h] * chunk_decay[h] + _dot_tn(kd, v)
            first[i] = (scores.astype(BF16), cross)
        if 0 <= i - RET_SKEW < n:
            rows, qk, vv = slices(items[i - RET_SKEW])
            scores, cross = first.pop(i - RET_SKEW)
            second[i - RET_SKEW] = _dot(scores, v_ref[rows, vv]) + cross
        if 0 <= i - 2 * RET_SKEW < n:
            rows, qk, vv = slices(items[i - 2 * RET_SKEW])
            y = second.pop(i - 2 * RET_SKEW)
            mu = jnp.mean(y, axis=-1, keepdims=True)
            yc = y - mu
            var = jnp.mean(yc * yc, axis=-1, keepdims=True)
            o_ref[rows, vv] = yc * lax.rsqrt(var + GN_EPS)
    for h in range(RET_HEADS):
        state_ref[h] = states[h]


def _retention(pa, decay, qdec, kdec, chunk_decay, *, batch, seq, chunks=8):
    rows = chunks * RET_CHUNK
    steps = seq // rows
    m = pa.shape[0]
    row = lambda b, n: b * steps + n
    return pl.pallas_call(
        functools.partial(_retention_kernel, chunk_decay=chunk_decay, chunks=chunks),
        out_shape=jax.ShapeDtypeStruct((m, RET_V), F32),
        grid=(batch, steps),
        in_specs=[
            pl.BlockSpec((rows, RET_QK), lambda b, n: (row(b, n), 0)),
            pl.BlockSpec((rows, RET_QK), lambda b, n: (row(b, n), 1)),
            pl.BlockSpec((rows, RET_V), lambda b, n: (row(b, n), 1)),
            pl.BlockSpec((RET_HEADS, RET_CHUNK, RET_CHUNK), lambda b, n: (0, 0, 0)),
            pl.BlockSpec((RET_CHUNK, RET_V), lambda b, n: (0, 0)),
            pl.BlockSpec((RET_CHUNK, RET_QK), lambda b, n: (0, 0)),
        ],
        out_specs=pl.BlockSpec((rows, RET_V), lambda b, n: (row(b, n), 0)),
        scratch_shapes=[pltpu.VMEM((RET_HEADS, RET_QK_DIM, RET_V_DIM), F32)],
        compiler_params=_params(("parallel", "arbitrary")),
        name="retention",
    )(pa, pa, pa, decay, qdec, kdec)


LOG2_E = 1.4426950408889634
SB_EXP2_UNDERFLOW = -150.0
SKEW_A = 2
SKEW_B = 4


def _sb_logits(q, kb):
    return _dot_nt(q, kb) * (SB_HEAD_DIM ** -0.5 * LOG2_E)


def _sb_survival(u, tri, mask):
    softplus = jnp.log2(1.0 + jnp.exp2(-jnp.abs(u)))
    log_beta = jnp.minimum(u, 0.0) - softplus
    log_1m = log_beta - u
    if mask is not None:
        log_1m = jnp.where(mask, log_1m, 0.0)
    hi = log_1m.astype(BF16)
    lo = (log_1m - hi.astype(F32)).astype(BF16)
    after = _dot(hi, tri) + _dot(lo, tri)
    return log_beta + after, jnp.sum(log_1m, axis=-1, keepdims=True)


def _sb_kernel(q_ref, k_ref, v_ref, tri_ref, o_ref, *, tb, heads):
    ti = pl.program_id(2)
    tri = tri_ref[...]
    rows = lax.broadcasted_iota(jnp.int32, (tb, tb), 0)
    cols = lax.broadcasted_iota(jnp.int32, (tb, tb), 1)
    causal = cols < rows
    start = pl.multiple_of(ti * tb, tb)

    def head(g):
        return slice(g * SB_HEAD_DIM, (g + 1) * SB_HEAD_DIM)

    def full_tile(g, j, carry, acc):
        off = pl.multiple_of(j * tb, tb)
        e, s = _sb_survival(_sb_logits(q_ref[:, head(g)], k_ref[pl.ds(off, tb), head(g)]),
                            tri, None)
        w = jnp.exp2(e + carry)
        return carry + s, acc + _dot(w.astype(BF16), v_ref[pl.ds(off, tb), head(g)])

    def static_tiles(previous):
        tiles = [(g, kind) for g in range(heads) for kind in (("D", "P") if previous else ("D",))]
        n = len(tiles)
        offset = {"D": start, "P": pl.multiple_of((ti - 1) * tb, tb)}
        logits, expo, carry, acc = {}, {}, {}, {}
        for i in range(n + SKEW_B):
            if i < n:
                g, kind = tiles[i]
                logits[i] = _sb_logits(q_ref[:, head(g)],
                                       k_ref[pl.ds(offset[kind], tb), head(g)])
            if 0 <= i - SKEW_A < n:
                mask = causal if tiles[i - SKEW_A][1] == "D" else None
                expo[i - SKEW_A] = _sb_survival(logits.pop(i - SKEW_A), tri, mask)
            if 0 <= i - SKEW_B < n:
                g, kind = tiles[i - SKEW_B]
                e, s = expo.pop(i - SKEW_B)
                vb = v_ref[pl.ds(offset[kind], tb), head(g)]
                if kind == "D":
                    w = jnp.where(causal, jnp.exp2(e), 0.0)
                    acc[g] = _dot(w.astype(BF16), vb)
                    carry[g] = s
                else:
                    w = jnp.exp2(e + carry[g])
                    acc[g] = acc[g] + _dot(w.astype(BF16), vb)
                    carry[g] = carry[g] + s
        return [(carry[g], acc[g]) for g in range(heads)]

    @pl.when(ti == 0)
    def _():
        for g, (_, acc) in enumerate(static_tiles(previous=False)):
            o_ref[:, head(g)] = acc.astype(o_ref.dtype)

    @pl.when(ti > 0)
    def _():
        def alive(carry):
            return jnp.max(carry) > SB_EXP2_UNDERFLOW

        state = static_tiles(previous=True)
        flags = [alive(carry) for carry, _ in state]
        for g, (carry, acc) in enumerate(state):
            def cond(c):
                return jnp.logical_and(c[0] >= 0, c[3])

            def body(c, g=g):
                carry, acc = full_tile(g, c[0], c[1], c[2])
                return c[0] - 1, carry, acc, alive(carry)

            _, _, acc, _ = lax.while_loop(cond, body, (ti - 2, carry, acc, flags[g]))
            o_ref[:, head(g)] = acc.astype(o_ref.dtype)


def _stick_breaking(pa, tri, *, batch, seq, tb=256, heads=8):
    m = pa.shape[0]
    t_blocks = seq // tb
    width = heads * SB_HEAD_DIM
    q_col = (2 * RET_QK + RET_V) // width
    k_col = q_col + SB_W // width
    v_col = k_col + SB_W // width
    return pl.pallas_call(
        functools.partial(_sb_kernel, tb=tb, heads=heads),
        out_shape=jax.ShapeDtypeStruct((m, SB_W), BF16),
        grid=(batch, SB_HEADS // heads, t_blocks),
        in_specs=[
            pl.BlockSpec((tb, width), lambda b, h, t: (b * t_blocks + t, q_col + h)),
            pl.BlockSpec((seq, width), lambda b, h, t: (b, k_col + h)),
            pl.BlockSpec((seq, width), lambda b, h, t: (b, v_col + h)),
            pl.BlockSpec((tb, tb), lambda b, h, t: (0, 0)),
        ],
        out_specs=pl.BlockSpec((tb, width), lambda b, h, t: (b * t_blocks + t, h)),
        compiler_params=_params(("parallel", "parallel", "arbitrary")),
        name="stick_breaking",
    )(pa, pa, pa, tri)


MERGE_ROWS = 256


def _merge_kernel(ret_ref, sb_ref, x_ref, g_ref, wsw_ref, wg_ref, pr_ref, ps_ref, wo_ref, o_ref):
    tm, d = x_ref.shape
    gain = g_ref[...]
    chunks = [slice(c * MERGE_ROWS, (c + 1) * MERGE_ROWS) for c in range(tm // MERGE_ROWS)]
    hs = [_rms(x_ref[r, :], gain).astype(BF16) for r in chunks]
    swish = [_dot(h, wsw_ref[...]) for h in hs]
    gates = [_dot(h, wg_ref[...]) for h in hs]
    branch_b = [_dot(sb_ref[r, :], ps_ref[...]) for r in chunks]
    gated = [(ret_ref[r, :] * (s * jax.nn.sigmoid(s))).astype(BF16) for r, s in zip(chunks, swish)]
    branch_a = [_dot(g, pr_ref[...]) for g in gated]
    merged = [(jax.nn.sigmoid(gt[:, :d]) * a + jax.nn.sigmoid(gt[:, d:]) * b).astype(BF16)
              for gt, a, b in zip(gates, branch_a, branch_b)]
    for r, mg in zip(chunks, merged):
        o_ref[r, :] = x_ref[r, :] + _dot(mg, wo_ref[...])


def _merge(ret, sb, x2, gain, w_in, p_ret, p_sb, w_out, layer, *, tm=512):
    m, d = x2.shape
    row = lambda i: (i, 0)
    per_layer = lambda i: (layer, 0, 0)
    swish_col = (2 * RET_QK + RET_V) // RET_V
    gate_col = (2 * RET_QK + 2 * RET_V + 3 * SB_W) // (2 * d)

    def weight(shape, index_map):
        return pl.BlockSpec(shape, index_map, pipeline_mode=pl.Buffered(1))

    return pl.pallas_call(
        _merge_kernel,
        out_shape=jax.ShapeDtypeStruct((m, d), F32),
        grid=(m // tm,),
        in_specs=[
            pl.BlockSpec((tm, RET_V), row),
            pl.BlockSpec((tm, SB_W), row),
            pl.BlockSpec((tm, d), row),
            pl.BlockSpec((None, 1, d), per_layer),
            weight((None, d, RET_V), lambda i: (layer, 0, swish_col)),
            weight((None, d, 2 * d), lambda i: (layer, 0, gate_col)),
            weight((None, RET_V, d), per_layer),
            weight((None, SB_W, d), per_layer),
            weight((None, d, d), per_layer),
        ],
        out_specs=pl.BlockSpec((tm, d), row),
        compiler_params=_params(("parallel",)),
        name="gate_merge",
    )(ret, sb, x2, gain, w_in, w_in, p_ret, p_sb, w_out)


def _mlp_kernel(x_ref, g_ref, wu_ref, wd_ref, gf_ref, o_ref, h_ref, *, final_norm):
    j = pl.program_id(1)

    def mlp(h):
        up = jnp.maximum(_dot(h, wu_ref[...]), 0.0)
        return _dot((up * up).astype(BF16), wd_ref[...])

    @pl.when(j == 0)
    def _():
        gain = g_ref[...]
        n_chunks = x_ref.shape[0] // NORM_ROWS

        def rows(c):
            return slice(c * NORM_ROWS, (c + 1) * NORM_ROWS)

        h_ref[rows(0), :] = _rms(x_ref[rows(0), :], gain).astype(BF16)
        for c in range(n_chunks):
            y = mlp(h_ref[rows(c), :])
            if c + 1 < n_chunks:
                h_ref[rows(c + 1), :] = _rms(x_ref[rows(c + 1), :], gain).astype(BF16)
            o_ref[rows(c), :] = x_ref[rows(c), :] + y

    @pl.when(j != 0)
    def _():
        o_ref[...] += mlp(h_ref[...])

    if final_norm:
        @pl.when(j == pl.num_programs(1) - 1)
        def _():
            gain = gf_ref[...]

            def rows(c, carry):
                r = pl.ds(pl.multiple_of(c * NORM_ROWS, NORM_ROWS), NORM_ROWS)
                o_ref[r, :] = _rms(o_ref[r, :], gain)
                return carry

            lax.fori_loop(0, o_ref.shape[0] // NORM_ROWS, rows, 0)


def _mlp(x2, gain, w_up, w_down, g_final, layer, *, final_norm, tm=1024, tf=2048):
    m, d = x2.shape
    ff = w_up.shape[2]
    return pl.pallas_call(
        functools.partial(_mlp_kernel, final_norm=final_norm),
        out_shape=jax.ShapeDtypeStruct((m, d), F32),
        grid=(m // tm, ff // tf),
        in_specs=[
            pl.BlockSpec((tm, d), lambda i, j: (i, 0)),
            pl.BlockSpec((None, 1, d), lambda i, j: (layer, 0, 0)),
            pl.BlockSpec((None, d, tf), lambda i, j: (layer, 0, j)),
            pl.BlockSpec((None, tf, d), lambda i, j: (layer, j, 0)),
            pl.BlockSpec((1, d), lambda i, j: (0, 0)),
        ],
        out_specs=pl.BlockSpec((tm, d), lambda i, j: (i, 0)),
        scratch_shapes=[pltpu.VMEM((tm, d), BF16)],
        compiler_params=_params(("parallel", "arbitrary")),
        name="relu2_mlp_final" if final_norm else "relu2_mlp",
    )(x2, gain, w_up, w_down, g_final)


def _tables(seq, tb):
    half = RET_QK_DIM // 2
    inv_freq = ROPE_BASE ** (-jnp.arange(half, dtype=F32) / half)
    ang = jnp.arange(seq, dtype=F32)[:, None] * inv_freq[None, :]
    cos = jnp.cos(ang)
    sin = jnp.sin(ang)
    cos_t = jnp.concatenate([cos, cos], axis=-1)
    sin_t = jnp.concatenate([-sin, sin], axis=-1)

    log_gamma = jnp.log1p(-(2.0 ** (-5.0 - jnp.arange(RET_HEADS, dtype=F32))))
    c = RET_CHUNK
    idx = jnp.arange(c, dtype=F32)
    diff = idx[:, None] - idx[None, :]
    decay = jnp.where(diff[None] >= 0,
                      jnp.exp(log_gamma[:, None, None] * jnp.maximum(diff, 0.0)[None]), 0.0)
    q_decay = jnp.exp(log_gamma[None, :] * (idx[:, None] + 1.0))
    k_decay = jnp.exp(log_gamma[None, :] * (c - 1.0 - idx[:, None]))
    qdec = jnp.repeat(q_decay, RET_V_DIM, axis=1)
    kdec = jnp.repeat(k_decay, RET_QK_DIM, axis=1)
    gamma = 1.0 - 2.0 ** (-5.0 - np.arange(RET_HEADS, dtype=np.float64))
    chunk_decay = tuple(float(g) for g in gamma ** c)

    s_idx = jnp.arange(tb)
    tri = (s_idx[:, None] > s_idx[None, :]).astype(BF16)
    return cos_t, sin_t, decay, qdec, kdec, chunk_decay, tri


def kernel(x, w_in, p_ret, p_sb, w_out, w_up, w_down, g_mix, g_mlp, g_final):
    batch, seq, d = x.shape
    depth = w_in.shape[0]
    sb_tb = 256
    cos_t, sin_t, decay, qdec, kdec, chunk_decay, tri = _tables(seq, sb_tb)

    w_in_b = w_in.astype(BF16)
    p_ret_b = p_ret.astype(BF16)
    p_sb_b = p_sb.astype(BF16)
    w_out_b = w_out.astype(BF16)
    w_up_b = w_up.astype(BF16)
    w_down_b = w_down.astype(BF16)
    g_mix3 = g_mix.reshape(depth, 1, d)
    g_mlp3 = g_mlp.reshape(depth, 1, d)

    x2 = x.reshape(batch * seq, d)
    g_fin = g_final.reshape(1, d)
    for l in range(depth):
        pa = _norm_proj(x2, g_mix3, w_in_b, l, cos_t, sin_t)
        ret = _retention(pa, decay, qdec, kdec, chunk_decay, batch=batch, seq=seq)
        sb = _stick_breaking(pa, tri, batch=batch, seq=seq, tb=sb_tb)
        x2 = _merge(ret, sb, x2, g_mix3, w_in_b, p_ret_b, p_sb_b, w_out_b, l)
        x2 = _mlp(x2, g_mlp3, w_up_b, w_down_b, g_fin, l, final_norm=(l == depth - 1))
    return x2.reshape(batch, seq, d)
```
